```python
import jax, jax.numpy as jnp
from jax import lax
import numpy as np

D_MODEL = 1024
BATCH = 8
SEQ = 2048
DEPTH = 1
DEC_BATCH = 128
DEC_SEQ = 8
PAST_LEN = 8192
PAGE_SIZE = 128

N_HEADS = 16
N_KV_HEADS = 4
HEAD_DIM = D_MODEL // N_HEADS
GROUP = N_HEADS // N_KV_HEADS
WINDOW = 128
Q_DIM = N_HEADS * HEAD_DIM
KV_DIM = N_KV_HEADS * HEAD_DIM
CONV_CH = D_MODEL
CONV_WIDTH = 31
D_FF = 2816
FFN_CONV_WIDTH = 3
EPS = 1e-6
NEG_INF = -1e30
SPLITS = (Q_DIM, KV_DIM, KV_DIM, CONV_CH, CONV_CH, D_MODEL, D_MODEL)
IN_DIM = sum(SPLITS)
SPLIT_IDX = tuple(int(i) for i in np.cumsum(SPLITS)[:-1])

kernel_name = 'hybrid_swa_sink_conformer_convffn_step'


def rmsnorm(x, g):
    xf = x.astype(jnp.float32)
    y = xf * lax.rsqrt(jnp.mean(xf * xf, axis=-1, keepdims=True) + EPS)
    return (y * g.astype(jnp.float32)).astype(x.dtype)


def layernorm(x, g, b):
    xf = x.astype(jnp.float32)
    mu = jnp.mean(xf, axis=-1, keepdims=True)
    var = jnp.mean(jnp.square(xf - mu), axis=-1, keepdims=True)
    y = (xf - mu) * lax.rsqrt(var + EPS)
    return (y * g.astype(jnp.float32) + b.astype(jnp.float32)).astype(x.dtype)


def dwconv_valid(x, w, b):
    c = x.shape[-1]
    y = lax.conv_general_dilated(x, w.astype(x.dtype)[:, None, :], window_strides=(1,), padding='VALID',
                                 dimension_numbers=('NWC', 'WIO', 'NWC'), feature_group_count=c)
    return y + b.astype(x.dtype)


def attend(q, k, v, mask, sinks):
    s = jnp.einsum('bnqkgd,bnskd->bnkgqs', q, k).astype(jnp.float32) * (HEAD_DIM ** -0.5)
    s = jnp.where(mask[None, :, None, None], s, NEG_INF)
    sink = sinks.astype(jnp.float32).reshape(N_KV_HEADS, GROUP)[None, None, :, :, None, None]
    sink = jnp.broadcast_to(sink, s.shape[:-1] + (1,))
    p = jax.nn.softmax(jnp.concatenate([s, sink], axis=-1), axis=-1)[..., :-1]
    return jnp.einsum('bnkgqs,bnskd->bnqkgd', p.astype(v.dtype), v)


def swa_prompt(q, k, v, sinks):
    b, s = q.shape[:2]
    nb = s // WINDOW
    qb = q.reshape(b, nb, WINDOW, N_KV_HEADS, GROUP, HEAD_DIM)
    kb = k.reshape(b, nb, WINDOW, N_KV_HEADS, HEAD_DIM)
    vb = v.reshape(b, nb, WINDOW, N_KV_HEADS, HEAD_DIM)
    kk = jnp.concatenate([jnp.concatenate([jnp.zeros_like(kb[:, :1]), kb[:, :-1]], axis=1), kb], axis=2)
    vv = jnp.concatenate([jnp.concatenate([jnp.zeros_like(vb[:, :1]), vb[:, :-1]], axis=1), vb], axis=2)
    i = jnp.arange(WINDOW)[:, None]
    j = jnp.arange(2 * WINDOW)[None, :]
    delta = i + WINDOW - j
    band = (delta >= 0) & (delta < WINDOW)
    blk = jnp.arange(nb)[:, None, None]
    mask = band[None] & (blk * WINDOW + j[None] - WINDOW >= 0)
    o = attend(qb, kk, vv, mask, sinks).reshape(b, s, Q_DIM)
    return o, k[:, -WINDOW:], v[:, -WINDOW:]


def swa_sample(q, k, v, k_past, v_past, sinks):
    b, t = q.shape[:2]
    kk = jnp.concatenate([k_past, k], axis=1)
    vv = jnp.concatenate([v_past, v], axis=1)
    i = jnp.arange(t)[:, None]
    j = jnp.arange(WINDOW + t)[None, :]
    delta = i + WINDOW - j
    mask = ((delta >= 0) & (delta < WINDOW))[None]
    qb = q.reshape(b, 1, t, N_KV_HEADS, GROUP, HEAD_DIM)
    o = attend(qb, kk[:, None], vv[:, None], mask, sinks).reshape(b, t, Q_DIM)
    return o, kk[:, -WINDOW:], vv[:, -WINDOW:]


def hybrid_layer(x, k_past, v_past, conv_b_past, conv_f_past, norm1_g, w_in, sinks, w_o_attn,
                 conv_b_w, conv_b_bias, ln_b_g, ln_b_b, w_pb, w_out, norm2_g, w_up,
                 ffn_conv_w, ffn_conv_b, w_down, is_prompt):
    b, t, _ = x.shape
    xn = rmsnorm(x, norm1_g)
    proj = xn @ w_in
    q, k, v, glu_a, glu_b, gate_a, gate_b = jnp.split(proj, SPLIT_IDX, axis=-1)
    q = q.reshape(b, t, N_HEADS, HEAD_DIM)
    k = k.reshape(b, t, N_KV_HEADS, HEAD_DIM)
    v = v.reshape(b, t, N_KV_HEADS, HEAD_DIM)
    if is_prompt:
        attn, k_new, v_new = swa_prompt(q, k, v, sinks)
    else:
        attn, k_new, v_new = swa_sample(q, k, v, k_past, v_past, sinks)
    branch_a = attn @ w_o_attn
    u = glu_a * jax.nn.sigmoid(glu_b)
    if is_prompt:
        conv_b_past = jnp.zeros((b, CONV_WIDTH - 1, CONV_CH), u.dtype)
    u_ext = jnp.concatenate([conv_b_past, u], axis=1)
    conv_b_new = u_ext[:, -(CONV_WIDTH - 1):]
    c = dwconv_valid(u_ext, conv_b_w, conv_b_bias)
    branch_b = jax.nn.silu(layernorm(c, ln_b_g, ln_b_b)) @ w_pb
    merged = jax.nn.sigmoid(gate_a) * branch_a + jax.nn.sigmoid(gate_b) * branch_b
    h = x + merged @ w_out
    up = rmsnorm(h, norm2_g) @ w_up
    if is_prompt:
        conv_f_past = jnp.zeros((b, FFN_CONV_WIDTH - 1, 2 * D_FF), up.dtype)
    up_ext = jnp.concatenate([conv_f_past, up], axis=1)
    conv_f_new = up_ext[:, -(FFN_CONV_WIDTH - 1):]
    uc = dwconv_valid(up_ext, ffn_conv_w, ffn_conv_b)
    gate, val = jnp.split(uc, 2, axis=-1)
    y = h + (jax.nn.silu(gate) * val) @ w_down
    return y, k_new, v_new, conv_b_new, conv_f_new


def setup_inputs(seed: int = 0) -> dict:
    key = jax.random.key(seed)
    ks = jax.random.split(key, 24)
    f32 = jnp.float32
    nrm = lambda k, shape, scale: jax.random.normal(k, shape, f32) * scale
    return {
        'x_prompt': nrm(ks[0], (BATCH, SEQ, D_MODEL), 1.0),
        'x_sample': nrm(ks[1], (DEC_BATCH, DEC_SEQ, D_MODEL), 1.0),
        'cache_k_win': nrm(ks[2], (DEPTH, DEC_BATCH, WINDOW, N_KV_HEADS, HEAD_DIM), 1.0),
        'cache_v_win': nrm(ks[3], (DEPTH, DEC_BATCH, WINDOW, N_KV_HEADS, HEAD_DIM), 1.0),
        'state_conv_b': nrm(ks[4], (DEPTH, DEC_BATCH, CONV_WIDTH - 1, CONV_CH), 0.5),
        'state_conv_ffn': nrm(ks[5], (DEPTH, DEC_BATCH, FFN_CONV_WIDTH - 1, 2 * D_FF), 0.5),
        'norm1_g': 1.0 + nrm(ks[6], (DEPTH, D_MODEL), 0.02),
        'w_in': nrm(ks[7], (DEPTH, D_MODEL, IN_DIM), D_MODEL ** -0.5),
        'sinks': nrm(ks[8], (DEPTH, N_HEADS), 0.5),
        'w_o_attn': nrm(ks[9], (DEPTH, Q_DIM, D_MODEL), Q_DIM ** -0.5),
        'conv_b_w': nrm(ks[10], (DEPTH, CONV_WIDTH, CONV_CH), CONV_WIDTH ** -0.5),
        'conv_b_bias': nrm(ks[11], (DEPTH, CONV_CH), 0.02),
        'ln_b_g': 1.0 + nrm(ks[12], (DEPTH, CONV_CH), 0.02),
        'ln_b_b': nrm(ks[13], (DEPTH, CONV_CH), 0.02),
        'w_pb': nrm(ks[14], (DEPTH, CONV_CH, D_MODEL), CONV_CH ** -0.5),
        'w_out': nrm(ks[15], (DEPTH, D_MODEL, D_MODEL), D_MODEL ** -0.5),
        'norm2_g': 1.0 + nrm(ks[16], (DEPTH, D_MODEL), 0.02),
        'w_up': nrm(ks[17], (DEPTH, D_MODEL, 2 * D_FF), D_MODEL ** -0.5),
        'ffn_conv_w': nrm(ks[18], (DEPTH, FFN_CONV_WIDTH, 2 * D_FF), FFN_CONV_WIDTH ** -0.5),
        'ffn_conv_b': nrm(ks[19], (DEPTH, 2 * D_FF), 0.02),
        'w_down': nrm(ks[20], (DEPTH, D_FF, D_MODEL), D_FF ** -0.5),
        'norm_f_g': 1.0 + nrm(ks[21], (D_MODEL,), 0.02),
    }


def reference(x_prompt, x_sample, cache_k_win, cache_v_win, state_conv_b, state_conv_ffn,
              norm1_g, w_in, sinks, w_o_attn, conv_b_w, conv_b_bias, ln_b_g, ln_b_b, w_pb, w_out,
              norm2_g, w_up, ffn_conv_w, ffn_conv_b, w_down, norm_f_g):
    hp, hs = x_prompt, x_sample
    kp_l, vp_l, cbp_l, cfp_l = [], [], [], []
    ks_l, vs_l, cbs_l, cfs_l = [], [], [], []
    for l in range(DEPTH):
        weights = (norm1_g[l], w_in[l], sinks[l], w_o_attn[l], conv_b_w[l], conv_b_bias[l], ln_b_g[l],
                   ln_b_b[l], w_pb[l], w_out[l], norm2_g[l], w_up[l], ffn_conv_w[l], ffn_conv_b[l], w_down[l])
        hp, kp, vp, cbp, cfp = hybrid_layer(hp, None, None, None, None, *weights, is_prompt=True)
        hs, kk, vv, cbs, cfs = hybrid_layer(hs, cache_k_win[l], cache_v_win[l], state_conv_b[l],
                                            state_conv_ffn[l], *weights, is_prompt=False)
        kp_l.append(kp); vp_l.append(vp); cbp_l.append(cbp); cfp_l.append(cfp)
        ks_l.append(kk); vs_l.append(vv); cbs_l.append(cbs); cfs_l.append(cfs)
    y_prompt = rmsnorm(hp, norm_f_g)
    y_sample = rmsnorm(hs, norm_f_g)
    return (y_prompt, y_sample,
            jnp.stack(kp_l), jnp.stack(vp_l), jnp.stack(cbp_l), jnp.stack(cfp_l),
            jnp.stack(ks_l), jnp.stack(vs_l), jnp.stack(cbs_l), jnp.stack(cfs_l))
```

```python
import functools

import jax
import jax.numpy as jnp
from jax import lax
from jax.experimental import pallas as pl
from jax.experimental.pallas import tpu as pltpu

F32 = jnp.float32
BF16 = jnp.bfloat16

D = 1024
N_HEADS, N_KV, HEAD_DIM = 16, 4, 64
WINDOW = 128
KV_DIM = N_KV * HEAD_DIM
CONV_W = 31
D_FF = 2816
FFN_CONV_W = 3
EPS = 1e-6
NEG_INF = -1e30
SCALE = HEAD_DIM ** -0.5
QKV_W = D + 2 * KV_DIM
GLU_OFF = QKV_W
GATE_OFF = QKV_W + 2 * D
IN_DIM = QKV_W + 4 * D
LANES = 128
HIST = 32
FFN_CH = 256
FFN_NC = D_FF // FFN_CH
VMEM_LIMIT = 56 * 1024 * 1024


def _resident(shape):
    nd = len(shape)
    return pl.BlockSpec(shape, lambda *_: (0,) * nd, pipeline_mode=pl.Buffered(1))


def _sigmoid(x):
    return 1.0 / (1.0 + jnp.exp(-x))


def _rmsnorm(x, g):
    return x * lax.rsqrt(jnp.mean(x * x, axis=-1, keepdims=True) + EPS) * g


def _layernorm(x, g, b):
    mu = jnp.mean(x, axis=-1, keepdims=True)
    xc = x - mu
    var = jnp.mean(xc * xc, axis=-1, keepdims=True)
    return xc * lax.rsqrt(var + EPS) * g + b


def _dot(a, b):
    return jnp.dot(a, b, preferred_element_type=F32)


def _softmax_with_sink(s, mask, sink):
    s = jnp.where(mask, s, NEG_INF)
    m = jnp.maximum(jnp.max(s, axis=-1, keepdims=True), sink)
    e = jnp.exp(s - m)
    den = jnp.sum(e, axis=-1, keepdims=True) + jnp.exp(sink - m)
    return e / den


def _split_heads_block_diag(x, lo):
    xr = pltpu.roll(x, HEAD_DIM, axis=1)
    z = jnp.zeros_like(x)
    even = (jnp.where(lo, x, z), jnp.where(lo, z, xr))
    odd = (jnp.where(lo, xr, z), jnp.where(lo, z, x))
    return even, odd


def _prompt_a_body(x_ref, g1_ref, w_in_ref, sinks_ref, w_o_ref, cw_ref, cb_ref, lng_ref, lnb_ref,
                   w_pb_ref, w_out_ref,
                   h_ref, kwin_ref, vwin_ref, cbs_ref,
                   kbd_scr, vbd_scr, q_scr, attn_scr, u_scr, c_scr, *, tm):
    t = pl.program_id(1)
    last = pl.num_programs(1) - 1

    @pl.when(t == 0)
    def _zero_history():
        kbd_scr[:, :, 0:WINDOW, :] = jnp.zeros((N_KV, 2, WINDOW, LANES), BF16)
        vbd_scr[:, :, 0:WINDOW, :] = jnp.zeros((N_KV, 2, WINDOW, LANES), BF16)
        u_scr[0:HIST, :] = jnp.zeros((HIST, D), F32)

    x = x_ref[0]
    xn = _rmsnorm(x, g1_ref[...]).astype(BF16)

    qkv = _dot(xn, w_in_ref[:, 0:QKV_W])
    q_scr[...] = qkv[:, 0:D].astype(BF16)
    k_new = qkv[:, D:D + KV_DIM]
    v_new = qkv[:, D + KV_DIM:QKV_W]

    @pl.when(t == last)
    def _emit_kv_window():
        kwin_ref[0] = k_new[tm - WINDOW:, :]
        vwin_ref[0] = v_new[tm - WINDOW:, :]

    lo = lax.broadcasted_iota(jnp.int32, (tm, LANES), 1) < HEAD_DIM
    for m in range(N_KV // 2):
        k_even, k_odd = _split_heads_block_diag(k_new[:, LANES * m:LANES * (m + 1)] * SCALE, lo)
        v_even, v_odd = _split_heads_block_diag(v_new[:, LANES * m:LANES * (m + 1)], lo)
        for half in range(2):
            kbd_scr[2 * m, half, WINDOW:, :] = k_even[half].astype(BF16)
            kbd_scr[2 * m + 1, half, WINDOW:, :] = k_odd[half].astype(BF16)
            vbd_scr[2 * m, half, WINDOW:, :] = v_even[half].astype(BF16)
            vbd_scr[2 * m + 1, half, WINDOW:, :] = v_odd[half].astype(BF16)

    row = lax.broadcasted_iota(jnp.int32, (WINDOW, 2 * WINDOW), 0)
    col = lax.broadcasted_iota(jnp.int32, (WINDOW, 2 * WINDOW), 1)
    band = jnp.logical_and(col > row, col <= row + WINDOW)

    def _query_block(qb, carry):
        r0 = pl.multiple_of(qb * WINDOW, WINDOW)
        first_key = jnp.where(jnp.logical_or(t > 0, qb > 0), 0, WINDOW)
        mask = jnp.logical_and(band, col >= first_key)
        for g in range(N_KV):
            kbd = jnp.concatenate([kbd_scr[g, 0, pl.ds(r0, 2 * WINDOW), :],
                                   kbd_scr[g, 1, pl.ds(r0, 2 * WINDOW), :]], axis=0)
            vbd = jnp.concatenate([vbd_scr[g, 0, pl.ds(r0, 2 * WINDOW), :],
                                   vbd_scr[g, 1, pl.ds(r0, 2 * WINDOW), :]], axis=0)
            for pair in range(2):
                c0 = 2 * LANES * g + LANES * pair
                qp = q_scr[pl.ds(r0, WINDOW), c0:c0 + LANES]
                s = lax.dot_general(qp, kbd, (((1,), (1,)), ((), ())), preferred_element_type=F32)
                probs = []
                for hh in range(2):
                    sink = sinks_ref[4 * g + 2 * pair + hh]
                    p = _softmax_with_sink(s[:, 2 * WINDOW * hh:2 * WINDOW * (hh + 1)], mask, sink)
                    probs.append(p.astype(BF16))
                o = _dot(jnp.concatenate(probs, axis=1), vbd)
                attn_scr[pl.ds(r0, WINDOW), c0:c0 + LANES] = o.astype(BF16)
        return carry

    lax.fori_loop(0, tm // WINDOW, _query_block, 0)
    kbd_scr[:, :, 0:WINDOW, :] = kbd_scr[:, :, tm:tm + WINDOW, :]
    vbd_scr[:, :, 0:WINDOW, :] = vbd_scr[:, :, tm:tm + WINDOW, :]
    branch_a = _dot(attn_scr[...], w_o_ref[...])

    glu = _dot(xn, w_in_ref[:, GLU_OFF:GATE_OFF])
    u_scr[HIST:HIST + tm, :] = glu[:, 0:D] * _sigmoid(glu[:, D:2 * D])

    rows = 32
    lane_chunk = 256

    for r0 in range(0, tm, rows):
        for lc in range(D // lane_chunk):
            sl = slice(lc * lane_chunk, (lc + 1) * lane_chunk)
            acc = jnp.broadcast_to(cb_ref[:, sl], (rows, lane_chunk))
            for j in range(CONV_W):
                off = r0 + HIST - (CONV_W - 1) + j
                acc = acc + u_scr[off:off + rows, sl] * cw_ref[j:j + 1, sl]
            c_scr[r0:r0 + rows, sl] = acc

    @pl.when(t == last)
    def _emit_conv_state():
        cbs_ref[0] = u_scr[HIST + tm - (CONV_W - 1):HIST + tm, :]

    u_scr[0:HIST, :] = u_scr[tm:tm + HIST, :]

    y = _layernorm(c_scr[...], lng_ref[...], lnb_ref[...])
    branch_b = _dot((y * _sigmoid(y)).astype(BF16), w_pb_ref[...])

    gates = _dot(xn, w_in_ref[:, GATE_OFF:IN_DIM])
    merged = _sigmoid(gates[:, 0:D]) * branch_a + _sigmoid(gates[:, D:2 * D]) * branch_b
    h_ref[0] = x + _dot(merged.astype(BF16), w_out_ref[...])


def _prompt_stage_a(x, g1, w_in, sinks, w_o, cw, cb, lng, lnb, w_pb, w_out, *, tm):
    b, s, _ = x.shape
    grid = (b, s // tm)
    row_block = pl.BlockSpec((1, tm, D), lambda i, t: (i, t, 0))
    per_seq = lambda r, c: pl.BlockSpec((1, r, c), lambda i, t: (i, 0, 0))
    return pl.pallas_call(
        functools.partial(_prompt_a_body, tm=tm),
        grid=grid,
        in_specs=[row_block, _resident((1, D)), _resident((D, IN_DIM)),
                  pl.BlockSpec(memory_space=pltpu.SMEM),
                  _resident((D, D)), _resident((CONV_W, D)), _resident((1, D)), _resident((1, D)),
                  _resident((1, D)), _resident((D, D)), _resident((D, D))],
        out_specs=[row_block, per_seq(WINDOW, KV_DIM), per_seq(WINDOW, KV_DIM), per_seq(CONV_W - 1, D)],
        out_shape=[jax.ShapeDtypeStruct((b, s, D), F32),
                   jax.ShapeDtypeStruct((b, WINDOW, KV_DIM), F32),
                   jax.ShapeDtypeStruct((b, WINDOW, KV_DIM), F32),
                   jax.ShapeDtypeStruct((b, CONV_W - 1, D), F32)],
        scratch_shapes=[pltpu.VMEM((N_KV, 2, WINDOW + tm, LANES), BF16),
                        pltpu.VMEM((N_KV, 2, WINDOW + tm, LANES), BF16),
                        pltpu.VMEM((tm, D), BF16),
                        pltpu.VMEM((tm, D), BF16),
                        pltpu.VMEM((HIST + tm, D), F32),
                        pltpu.VMEM((tm, D), F32)],
        compiler_params=pltpu.CompilerParams(dimension_semantics=("arbitrary", "arbitrary"),
                                             vmem_limit_bytes=VMEM_LIMIT),
        name="prompt_stage_a",
    )(x, g1, w_in, sinks, w_o, cw, cb, lng, lnb, w_pb, w_out)


def _ffn_chunks(hn, prev_rows, wg_ref, wv_ref, fcw_ref, fcb_ref, wd_ref, acc_scr):
    def _chunk(c, carry):
        halves = []
        for part, w_ref in enumerate((wg_ref, wv_ref)):
            up = _dot(hn, w_ref[c])
            prev1, prev2 = prev_rows(part, c, up)
            w = fcw_ref[part, c]
            halves.append(w[0:1, :] * prev2 + w[1:2, :] * prev1 + w[2:3, :] * up + fcb_ref[part, c])
        gate, val = halves
        act = (gate * _sigmoid(gate) * val).astype(BF16)
        acc_scr[...] += _dot(act, wd_ref[c])
        return carry

    lax.fori_loop(0, FFN_NC, _chunk, 0)


def _prompt_ffn_body(h_ref, g2_ref, wg_ref, wv_ref, fcw_ref, fcb_ref, wd_ref, gf_ref,
                     y_ref, state_ref, shift_scr, carry_scr, acc_scr, *, tm):
    t = pl.program_id(1)

    @pl.when(t == 0)
    def _zero_history():
        carry_scr[...] = jnp.zeros(carry_scr.shape, F32)

    h = h_ref[0]
    hn = _rmsnorm(h, g2_ref[...]).astype(BF16)
    acc_scr[...] = h

    def prev_rows(part, c, up):
        shift_scr[part, 0:8, :] = carry_scr[part, c]
        shift_scr[part, 8:8 + tm, :] = up
        carry_scr[part, c] = up[tm - 8:, :]
        state_ref[0, part, c] = up[tm - (FFN_CONV_W - 1):, :]
        return shift_scr[part, 7:7 + tm, :], shift_scr[part, 6:6 + tm, :]

    _ffn_chunks(hn, prev_rows, wg_ref, wv_ref, fcw_ref, fcb_ref, wd_ref, acc_scr)
    y_ref[0] = _rmsnorm(acc_scr[...], gf_ref[...])


def _ffn_weight_specs():
    return [_resident((1, D)),
            _resident((FFN_NC, D, FFN_CH)), _resident((FFN_NC, D, FFN_CH)),
            _resident((2, FFN_NC, FFN_CONV_W, FFN_CH)), _resident((2, FFN_NC, 1, FFN_CH)),
            _resident((FFN_NC, FFN_CH, D)), _resident((1, D))]


def _prompt_ffn(h, ffn_w, *, tm):
    b, s, _ = h.shape
    row_block = pl.BlockSpec((1, tm, D), lambda i, t: (i, t, 0))
    state_shape = (b, 2, FFN_NC, FFN_CONV_W - 1, FFN_CH)
    return pl.pallas_call(
        functools.partial(_prompt_ffn_body, tm=tm),
        grid=(b, s // tm),
        in_specs=[row_block] + _ffn_weight_specs(),
        out_specs=[row_block,
                   pl.BlockSpec((1,) + state_shape[1:], lambda i, t: (i, 0, 0, 0, 0))],
        out_shape=[jax.ShapeDtypeStruct((b, s, D), F32), jax.ShapeDtypeStruct(state_shape, F32)],
        scratch_shapes=[pltpu.VMEM((2, 8 + tm, FFN_CH), F32),
                        pltpu.VMEM((2, FFN_NC, 8, FFN_CH), F32),
                        pltpu.VMEM((tm, D), F32)],
        compiler_params=pltpu.CompilerParams(dimension_semantics=("arbitrary", "arbitrary"),
                                             vmem_limit_bytes=VMEM_LIMIT),
        name="prompt_ffn",
    )(h, *ffn_w)


def _sample_proj_body(x_ref, g1_ref, w_in_ref, q_ref, kv_ref, u_ref, gates_ref, *, n_seq, n_tok):
    x = jnp.concatenate([x_ref[:, i, :] for i in range(n_tok)], axis=0)
    xn = _rmsnorm(x, g1_ref[...]).astype(BF16)
    qkv = _dot(xn, w_in_ref[:, 0:QKV_W])
    q_ref[...] = qkv[:, 0:D]
    kv_ref[...] = qkv[:, D:QKV_W]
    glu = _dot(xn, w_in_ref[:, GLU_OFF:GATE_OFF])
    u_ref[...] = glu[:, 0:D] * _sigmoid(glu[:, D:2 * D])
    gates_ref[...] = _sigmoid(_dot(xn, w_in_ref[:, GATE_OFF:IN_DIM]))


def _sample_proj(x, g1, w_in, *, n_seq, n_tok):
    m = n_seq * n_tok
    full = lambda c: pl.BlockSpec((m, c), lambda i: (0, 0))
    return pl.pallas_call(
        functools.partial(_sample_proj_body, n_seq=n_seq, n_tok=n_tok),
        grid=(1,),
        in_specs=[pl.BlockSpec((n_seq, n_tok, D), lambda i: (0, 0, 0)), _resident((1, D)),
                  _resident((D, IN_DIM))],
        out_specs=[full(D), full(2 * KV_DIM), full(D), full(2 * D)],
        out_shape=[jax.ShapeDtypeStruct((m, D), F32), jax.ShapeDtypeStruct((m, 2 * KV_DIM), F32),
                   jax.ShapeDtypeStruct((m, D), F32), jax.ShapeDtypeStruct((m, 2 * D), F32)],
        compiler_params=pltpu.CompilerParams(dimension_semantics=("arbitrary",),
                                             vmem_limit_bytes=VMEM_LIMIT),
        name="sample_proj",
    )(x, g1, w_in)


def _sample_mix_body(q_ref, kv_ref, u_ref, kpast_ref, vpast_ref, cstate_ref, sink_ref,
                     cw_ref, cb_ref, lng_ref, lnb_ref,
                     attn_ref, cs_ref, kwin_ref, vwin_ref, cstate_out_ref,
                     kall_scr, vall_scr, qg_scr, o_scr, *, gs, n_tok):
    n_keys = 2 * WINDOW
    past_keep = WINDOW - n_tok

    @pl.when(pl.program_id(0) == 0)
    def _zero_padding():
        kall_scr[...] = jnp.zeros(kall_scr.shape, F32)
        vall_scr[...] = jnp.zeros(vall_scr.shape, F32)

    kwin_ref[:, 0:past_keep, :] = kpast_ref[:, n_tok:, :]
    vwin_ref[:, 0:past_keep, :] = vpast_ref[:, n_tok:, :]
    for i in range(n_tok):
        kwin_ref[:, past_keep + i, :] = kv_ref[i, :, 0:KV_DIM]
        vwin_ref[:, past_keep + i, :] = kv_ref[i, :, KV_DIM:2 * KV_DIM]

    qrow = lax.broadcasted_iota(jnp.int32, (N_HEADS // N_KV * n_tok, n_keys), 0) % n_tok
    key = lax.broadcasted_iota(jnp.int32, (N_HEADS // N_KV * n_tok, n_keys), 1)
    mask = jnp.logical_and(key > qrow, key <= qrow + WINDOW)[None]
    for g in range(N_KV):
        hsl = slice(g * HEAD_DIM, (g + 1) * HEAD_DIM)
        kall_scr[:, 0:WINDOW, :] = kpast_ref[:, :, hsl] * SCALE
        vall_scr[:, 0:WINDOW, :] = vpast_ref[:, :, hsl]
        for i in range(n_tok):
            kall_scr[:, WINDOW + i, :] = kv_ref[i, :, hsl] * SCALE
            vall_scr[:, WINDOW + i, :] = kv_ref[i, :, KV_DIM + g * HEAD_DIM:KV_DIM + (g + 1) * HEAD_DIM]
            for hh in range(N_HEADS // N_KV):
                h0 = (4 * g + hh) * HEAD_DIM
                qg_scr[:, hh * n_tok + i, :] = q_ref[i, :, h0:h0 + HEAD_DIM]
        qg = qg_scr[...].astype(BF16)
        s = jnp.einsum("nqd,nkd->nqk", qg, kall_scr[...].astype(BF16), preferred_element_type=F32)
        p = _softmax_with_sink(s, mask, sink_ref[g][None])
        o_scr[...] = jnp.einsum("nqk,nkd->nqd", p.astype(BF16), vall_scr[...].astype(BF16),
                                preferred_element_type=F32)
        for i in range(n_tok):
            for hh in range(N_HEADS // N_KV):
                h0 = (4 * g + hh) * HEAD_DIM
                attn_ref[i, :, h0:h0 + HEAD_DIM] = o_scr[:, hh * n_tok + i, :]

    n_state = CONV_W - 1

    def ext(r, sl):
        return cstate_ref[:, r, sl] if r < n_state else u_ref[r - n_state, :, sl]

    for r in range(n_state):
        src = r + n_tok
        cstate_out_ref[:, r, :] = ext(src, slice(None))

    lane_chunk = 256
    for i in range(n_tok):
        for lc in range(D // lane_chunk):
            sl = slice(lc * lane_chunk, (lc + 1) * lane_chunk)
            acc = jnp.broadcast_to(cb_ref[:, sl], (gs, lane_chunk))
            for j in range(CONV_W):
                acc = acc + ext(i + j, sl) * cw_ref[j:j + 1, sl]
            cs_ref[i, :, sl] = acc
    for i in range(n_tok):
        y = _layernorm(cs_ref[i], lng_ref[...], lnb_ref[...])
        cs_ref[i] = y * _sigmoid(y)


def _sample_mix(q, kv, u, kpast, vpast, cstate, sink_rows, cw, cb, lng, lnb, *, gs):
    n_tok, n_seq, _ = q.shape
    tok_block = lambda c: pl.BlockSpec((n_tok, gs, c), lambda s: (0, s, 0))
    seq_block = lambda r, c: pl.BlockSpec((gs, r, c), lambda s: (s, 0, 0))
    rows_per_group = N_HEADS // N_KV * n_tok
    return pl.pallas_call(
        functools.partial(_sample_mix_body, gs=gs, n_tok=n_tok),
        grid=(n_seq // gs,),
        in_specs=[tok_block(D), tok_block(2 * KV_DIM), tok_block(D),
                  seq_block(WINDOW, KV_DIM), seq_block(WINDOW, KV_DIM), seq_block(CONV_W - 1, D),
                  _resident((N_KV, rows_per_group, 1)),
                  _resident((CONV_W, D)), _resident((1, D)), _resident((1, D)), _resident((1, D))],
        out_specs=[tok_block(D), tok_block(D),
                   seq_block(WINDOW, KV_DIM), seq_block(WINDOW, KV_DIM), seq_block(CONV_W - 1, D)],
        out_shape=[jax.ShapeDtypeStruct((n_tok, n_seq, D), F32), jax.ShapeDtypeStruct((n_tok, n_seq, D), F32),
                   jax.ShapeDtypeStruct((n_seq, WINDOW, KV_DIM), F32),
                   jax.ShapeDtypeStruct((n_seq, WINDOW, KV_DIM), F32),
                   jax.ShapeDtypeStruct((n_seq, CONV_W - 1, D), F32)],
        scratch_shapes=[pltpu.VMEM((gs, 2 * WINDOW, HEAD_DIM), F32),
                        pltpu.VMEM((gs, 2 * WINDOW, HEAD_DIM), F32),
                        pltpu.VMEM((gs, rows_per_group, HEAD_DIM), F32),
                        pltpu.VMEM((gs, rows_per_group, HEAD_DIM), F32)],
        compiler_params=pltpu.CompilerParams(dimension_semantics=("arbitrary",),
                                             vmem_limit_bytes=VMEM_LIMIT),
        name="sample_mix",
    )(q, kv, u, kpast, vpast, cstate, sink_rows, cw, cb, lng, lnb)


def _sample_merge_body(x_ref, attn_ref, cs_ref, gates_ref, w_o_ref, w_pb_ref, w_out_ref, h_ref, *, n_tok):
    x = jnp.concatenate([x_ref[:, i, :] for i in range(n_tok)], axis=0)
    branch_a = _dot(attn_ref[...].astype(BF16), w_o_ref[...])
    branch_b = _dot(cs_ref[...].astype(BF16), w_pb_ref[...])
    merged = gates_ref[:, 0:D] * branch_a + gates_ref[:, D:2 * D] * branch_b
    h_ref[...] = x + _dot(merged.astype(BF16), w_out_ref[...])


def _sample_merge(x, attn, cs, gates, w_o, w_pb, w_out):
    n_seq, n_tok, _ = x.shape
    m = n_seq * n_tok
    full = lambda c: pl.BlockSpec((m, c), lambda i: (0, 0))
    return pl.pallas_call(
        functools.partial(_sample_merge_body, n_tok=n_tok),
        grid=(1,),
        in_specs=[pl.BlockSpec((n_seq, n_tok, D), lambda i: (0, 0, 0)), full(D), full(D), full(2 * D),
                  _resident((D, D)), _resident((D, D)), _resident((D, D))],
        out_specs=full(D),
        out_shape=jax.ShapeDtypeStruct((m, D), F32),
        compiler_params=pltpu.CompilerParams(dimension_semantics=("arbitrary",),
                                             vmem_limit_bytes=VMEM_LIMIT),
        name="sample_merge",
    )(x, attn, cs, gates, w_o, w_pb, w_out)


def _sample_ffn_body(h_ref, state_ref, g2_ref, wg_ref, wv_ref, fcw_ref, fcb_ref, wd_ref, gf_ref,
                     y_ref, state_out_ref, acc_scr, *, n_seq, n_tok):
    m = n_seq * n_tok
    h = h_ref[...]
    hn = _rmsnorm(h, g2_ref[...]).astype(BF16)
    acc_scr[...] = h

    def prev_rows(part, c, up):
        s0 = state_ref[0, part, c]
        s1 = state_ref[1, part, c]
        state_out_ref[0, part, c] = up[m - 2 * n_seq:m - n_seq, :]
        state_out_ref[1, part, c] = up[m - n_seq:, :]
        prev1 = jnp.concatenate([s1, up[:m - n_seq, :]], axis=0)
        prev2 = jnp.concatenate([s0, s1, up[:m - 2 * n_seq, :]], axis=0)
        return prev1, prev2

    _ffn_chunks(hn, prev_rows, wg_ref, wv_ref, fcw_ref, fcb_ref, wd_ref, acc_scr)
    y = _rmsnorm(acc_scr[...], gf_ref[...])
    for i in range(n_tok):
        y_ref[:, i, :] = y[i * n_seq:(i + 1) * n_seq, :]


def _sample_ffn(h, state, ffn_w, *, n_seq, n_tok):
    m = n_seq * n_tok
    state_shape = (FFN_CONV_W - 1, 2, FFN_NC, n_seq, FFN_CH)
    state_spec = pl.BlockSpec(state_shape, lambda i: (0, 0, 0, 0, 0))
    seq_major = pl.BlockSpec((n_seq, n_tok, D), lambda i: (0, 0, 0))
    return pl.pallas_call(
        functools.partial(_sample_ffn_body, n_seq=n_seq, n_tok=n_tok),
        grid=(1,),
        in_specs=[pl.BlockSpec((m, D), lambda i: (0, 0)), state_spec] + _ffn_weight_specs(),
        out_specs=[seq_major, state_spec],
        out_shape=[jax.ShapeDtypeStruct((n_seq, n_tok, D), F32), jax.ShapeDtypeStruct(state_shape, F32)],
        scratch_shapes=[pltpu.VMEM((m, D), F32)],
        compiler_params=pltpu.CompilerParams(dimension_semantics=("arbitrary",),
                                             vmem_limit_bytes=VMEM_LIMIT),
        name="sample_ffn",
    )(h, state, *ffn_w)


def _layer(x_prompt, x_sample, k_past, v_past, conv_b_past, conv_f_past,
           norm1_g, w_in, sinks, w_o_attn, conv_b_w, conv_b_bias, ln_b_g, ln_b_b, w_pb, w_out,
           norm2_g, w_up, ffn_conv_w, ffn_conv_b, w_down, norm_f_g, *, tm_a, tm_ffn, gs):
    n_seq, n_tok, _ = x_sample.shape
    row = lambda v: v.reshape(1, -1)
    g1, g2, gf = row(norm1_g), row(norm2_g), row(norm_f_g)
    cb, lng, lnb = row(conv_b_bias), row(ln_b_g), row(ln_b_b)
    w_in_b, w_o_b, w_pb_b, w_out_b = (w.astype(BF16) for w in (w_in, w_o_attn, w_pb, w_out))
    w_up_c = w_up.astype(BF16).reshape(D, 2, FFN_NC, FFN_CH).transpose(1, 2, 0, 3)
    ffn_w = (g2, w_up_c[0], w_up_c[1],
             ffn_conv_w.reshape(FFN_CONV_W, 2, FFN_NC, FFN_CH).transpose(1, 2, 0, 3),
             ffn_conv_b.reshape(2, FFN_NC, 1, FFN_CH),
             w_down.astype(BF16).reshape(FFN_NC, FFN_CH, D), gf)

    h_p, k_win_p, v_win_p, conv_b_p = _prompt_stage_a(
        x_prompt, g1, w_in_b, sinks, w_o_b, conv_b_w, cb, lng, lnb, w_pb_b, w_out_b, tm=tm_a)
    y_p, ffn_state_p = _prompt_ffn(h_p, ffn_w, tm=tm_ffn)
    b = x_prompt.shape[0]
    conv_f_p = ffn_state_p.transpose(0, 3, 1, 2, 4).reshape(b, FFN_CONV_W - 1, 2 * D_FF)

    q, kv, u, gates = _sample_proj(x_sample, g1, w_in_b, n_seq=n_seq, n_tok=n_tok)
    tok_major = lambda a: a.reshape(n_tok, n_seq, a.shape[-1])
    sink_rows = jnp.repeat(sinks.reshape(N_KV, N_HEADS // N_KV), n_tok, axis=1)[:, :, None]
    attn, cs, k_win_s, v_win_s, conv_b_s = _sample_mix(
        tok_major(q), tok_major(kv), tok_major(u),
        k_past.reshape(n_seq, WINDOW, KV_DIM), v_past.reshape(n_seq, WINDOW, KV_DIM), conv_b_past,
        sink_rows, conv_b_w, cb, lng, lnb, gs=gs)
    ffn_state = conv_f_past.reshape(n_seq, FFN_CONV_W - 1, 2, FFN_NC, FFN_CH).transpose(1, 2, 3, 0, 4)
    h_s = _sample_merge(x_sample, attn.reshape(n_seq * n_tok, D), cs.reshape(n_seq * n_tok, D), gates,
                        w_o_b, w_pb_b, w_out_b)
    y_s, ffn_state_s = _sample_ffn(h_s, ffn_state, ffn_w, n_seq=n_seq, n_tok=n_tok)
    conv_f_s = ffn_state_s.transpose(3, 0, 1, 2, 4).reshape(n_seq, FFN_CONV_W - 1, 2 * D_FF)

    kv_shape = lambda a: a.reshape(a.shape[0], WINDOW, N_KV, HEAD_DIM)
    return (y_p, y_s,
            kv_shape(k_win_p), kv_shape(v_win_p), conv_b_p, conv_f_p,
            kv_shape(k_win_s), kv_shape(v_win_s), conv_b_s, conv_f_s)


def kernel(x_prompt, x_sample, cache_k_win, cache_v_win, state_conv_b, state_conv_ffn, norm1_g, w_in, sinks,
           w_o_attn, conv_b_w, conv_b_bias, ln_b_g, ln_b_b, w_pb, w_out, norm2_g, w_up, ffn_conv_w,
           ffn_conv_b, w_down, norm_f_g):
    assert w_in.shape[0] == 1, "single-layer stack only"
    res = _layer(x_prompt, x_sample, cache_k_win[0], cache_v_win[0], state_conv_b[0], state_conv_ffn[0],
                 norm1_g[0], w_in[0], sinks[0], w_o_attn[0], conv_b_w[0], conv_b_bias[0], ln_b_g[0],
                 ln_b_b[0], w_pb[0], w_out[0], norm2_g[0], w_up[0], ffn_conv_w[0], ffn_conv_b[0],
                 w_down[0], norm_f_g, tm_a=256, tm_ffn=512, gs=16)
    return res[:2] + tuple(r[None] for r in res[2:])
```

```python
import functools

import jax
import jax.numpy as jnp
from jax import lax
from jax.experimental import pallas as pl
from jax.experimental.pallas import tpu as pltpu

F32 = jnp.float32
BF16 = jnp.bfloat16

D = 1024
N_HEADS, N_KV, HEAD_DIM = 16, 4, 64
WINDOW = 128
KV_DIM = N_KV * HEAD_DIM
CONV_W = 31
D_FF = 2816
FFN_CONV_W = 3
EPS = 1e-6
NEG_INF = -1e30
SCALE = HEAD_DIM ** -0.5
QKV_W = D + 2 * KV_DIM
GLU_OFF = QKV_W
GATE_OFF = QKV_W + 2 * D
IN_DIM = QKV_W + 4 * D
LANES = 128
SUBLANES = 8
HIST = 32
FFN_CH = 256
FFN_NC = D_FF // FFN_CH
VMEM_LIMIT = 56 * 1024 * 1024


def _resident(shape):
    nd = len(shape)
    return pl.BlockSpec(shape, lambda *_: (0,) * nd, pipeline_mode=pl.Buffered(1))


def _sigmoid(x):
    return 1.0 / (1.0 + jnp.exp(-x))


def _rmsnorm(x, g):
    return x * lax.rsqrt(jnp.mean(x * x, axis=-1, keepdims=True) + EPS) * g


def _layernorm(x, g, b):
    mu = jnp.mean(x, axis=-1, keepdims=True)
    xc = x - mu
    var = jnp.mean(xc * xc, axis=-1, keepdims=True)
    return xc * lax.rsqrt(var + EPS) * g + b


def _dot(a, b):
    return jnp.dot(a, b, preferred_element_type=F32)


def _softmax_with_sink(s, mask, sink):
    s = jnp.where(mask, s, NEG_INF)
    m = jnp.maximum(jnp.max(s, axis=-1, keepdims=True), sink)
    e = jnp.exp(s - m)
    den = jnp.sum(e, axis=-1, keepdims=True) + jnp.exp(sink - m)
    return e / den


def _split_heads_block_diag(x, lo):
    xr = pltpu.roll(x, HEAD_DIM, axis=1)
    z = jnp.zeros_like(x)
    even = (jnp.where(lo, x, z), jnp.where(lo, z, xr))
    odd = (jnp.where(lo, xr, z), jnp.where(lo, z, x))
    return even, odd


def _prompt_a_body(x_ref, g1_ref, w_in_ref, sinks_ref, w_o_ref, cw_ref, cb_ref, lng_ref, lnb_ref,
                   w_pb_ref, w_out_ref,
                   h_ref, kwin_ref, vwin_ref, cbs_ref,
                   kbd_scr, vbd_scr, q_scr, attn_scr, u_scr, ush_scr, c_scr, *, tm):
    t = pl.program_id(1)
    last = pl.num_programs(1) - 1

    @pl.when(t == 0)
    def _zero_history():
        kbd_scr[:, :, 0:WINDOW, :] = jnp.zeros((N_KV, 2, WINDOW, LANES), BF16)
        vbd_scr[:, :, 0:WINDOW, :] = jnp.zeros((N_KV, 2, WINDOW, LANES), BF16)
        u_scr[0:HIST, :] = jnp.zeros((HIST, D), F32)

    x = x_ref[0]
    xn = _rmsnorm(x, g1_ref[...]).astype(BF16)

    qkv = _dot(xn, w_in_ref[:, 0:QKV_W])
    q_scr[...] = qkv[:, 0:D].astype(BF16)
    k_new = qkv[:, D:D + KV_DIM]
    v_new = qkv[:, D + KV_DIM:QKV_W]

    @pl.when(t == last)
    def _emit_kv_window():
        kwin_ref[0] = k_new[tm - WINDOW:, :]
        vwin_ref[0] = v_new[tm - WINDOW:, :]

    lo = lax.broadcasted_iota(jnp.int32, (tm, LANES), 1) < HEAD_DIM
    for m in range(N_KV // 2):
        k_even, k_odd = _split_heads_block_diag(k_new[:, LANES * m:LANES * (m + 1)] * SCALE, lo)
        v_even, v_odd = _split_heads_block_diag(v_new[:, LANES * m:LANES * (m + 1)], lo)
        for half in range(2):
            kbd_scr[2 * m, half, WINDOW:, :] = k_even[half].astype(BF16)
            kbd_scr[2 * m + 1, half, WINDOW:, :] = k_odd[half].astype(BF16)
            vbd_scr[2 * m, half, WINDOW:, :] = v_even[half].astype(BF16)
            vbd_scr[2 * m + 1, half, WINDOW:, :] = v_odd[half].astype(BF16)

    row = lax.broadcasted_iota(jnp.int32, (WINDOW, 2 * WINDOW), 0)
    col = lax.broadcasted_iota(jnp.int32, (WINDOW, 2 * WINDOW), 1)
    band = jnp.logical_and(col > row, col <= row + WINDOW)
    first_mask = jnp.logical_and(band, col >= jnp.where(t > 0, 0, WINDOW))

    def scores(qb, g, pair):
        r0 = qb * WINDOW
        c0 = 2 * LANES * g + LANES * pair
        kbd = jnp.concatenate([kbd_scr[g, 0, r0:r0 + 2 * WINDOW, :],
                               kbd_scr[g, 1, r0:r0 + 2 * WINDOW, :]], axis=0)
        return lax.dot_general(q_scr[r0:r0 + WINDOW, c0:c0 + LANES], kbd, (((1,), (1,)), ((), ())),
                               preferred_element_type=F32)

    items = [(qb, g, pair) for qb in range(tm // WINDOW) for g in range(N_KV) for pair in range(2)]
    s_next = scores(*items[0])
    for k, (qb, g, pair) in enumerate(items):
        s = s_next
        if k + 1 < len(items):
            s_next = scores(*items[k + 1])
        r0 = qb * WINDOW
        c0 = 2 * LANES * g + LANES * pair
        mask = first_mask if qb == 0 else band
        probs = []
        for hh in range(2):
            sink = sinks_ref[4 * g + 2 * pair + hh]
            p = _softmax_with_sink(s[:, 2 * WINDOW * hh:2 * WINDOW * (hh + 1)], mask, sink)
            probs.append(p.astype(BF16))
        vbd = jnp.concatenate([vbd_scr[g, 0, r0:r0 + 2 * WINDOW, :],
                               vbd_scr[g, 1, r0:r0 + 2 * WINDOW, :]], axis=0)
        o = _dot(jnp.concatenate(probs, axis=1), vbd)
        attn_scr[r0:r0 + WINDOW, c0:c0 + LANES] = o.astype(BF16)

    kbd_scr[:, :, 0:WINDOW, :] = kbd_scr[:, :, tm:tm + WINDOW, :]
    vbd_scr[:, :, 0:WINDOW, :] = vbd_scr[:, :, tm:tm + WINDOW, :]
    branch_a = _dot(attn_scr[...], w_o_ref[...])

    glu = _dot(xn, w_in_ref[:, GLU_OFF:GATE_OFF])
    u_scr[HIST:HIST + tm, :] = glu[:, 0:D] * _sigmoid(glu[:, D:2 * D])

    n_ext = HIST + tm
    u_ext = u_scr[...]
    for s in range(1, SUBLANES):
        ush_scr[s - 1] = pltpu.roll(u_ext, n_ext - s, axis=0)

    rows = 32
    lane_chunk = 256

    first_off = HIST - (CONV_W - 1)
    groups = [[o for o in range(first_off, first_off + CONV_W) if o % SUBLANES == s] for s in range(SUBLANES)]

    def _conv_rows(i, carry):
        r0 = pl.multiple_of(i * rows, rows)
        for lc in range(D // lane_chunk):
            sl = slice(lc * lane_chunk, (lc + 1) * lane_chunk)
            acc = jnp.broadcast_to(cb_ref[:, sl], (rows, lane_chunk))
            for s, offs in enumerate(groups):
                base = offs[0] - s
                span = offs[-1] - offs[0] + rows
                src = u_scr if s == 0 else ush_scr.at[s - 1]
                win = src[pl.ds(pl.multiple_of(r0 + base, SUBLANES), span), sl]
                for o in offs:
                    w = jnp.concatenate([cw_ref[o - first_off, :, sl]] * (rows // SUBLANES), axis=0)
                    acc = acc + win[o - offs[0]:o - offs[0] + rows, :] * w
            c_scr[pl.ds(r0, rows), sl] = acc
        return carry

    lax.fori_loop(0, tm // rows, _conv_rows, 0)

    @pl.when(t == last)
    def _emit_conv_state():
        cbs_ref[0] = u_scr[HIST + tm - (CONV_W - 1):HIST + tm, :]

    u_scr[0:HIST, :] = u_scr[tm:tm + HIST, :]

    y = _layernorm(c_scr[...], lng_ref[...], lnb_ref[...])
    branch_b = _dot((y * _sigmoid(y)).astype(BF16), w_pb_ref[...])

    gates = _dot(xn, w_in_ref[:, GATE_OFF:IN_DIM])
    merged = _sigmoid(gates[:, 0:D]) * branch_a + _sigmoid(gates[:, D:2 * D]) * branch_b
    h_ref[0] = x + _dot(merged.astype(BF16), w_out_ref[...])


def _prompt_stage_a(x, g1, w_in, sinks, w_o, cw, cb, lng, lnb, w_pb, w_out, *, tm):
    b, s, _ = x.shape
    grid = (b, s // tm)
    row_block = pl.BlockSpec((1, tm, D), lambda i, t: (i, t, 0))
    per_seq = lambda r, c: pl.BlockSpec((1, r, c), lambda i, t: (i, 0, 0))
    return pl.pallas_call(
        functools.partial(_prompt_a_body, tm=tm),
        grid=grid,
        in_specs=[row_block, _resident((1, D)), _resident((D, IN_DIM)),
                  pl.BlockSpec(memory_space=pltpu.SMEM),
                  _resident((D, D)), _resident((CONV_W, SUBLANES, D)), _resident((1, D)), _resident((1, D)),
                  _resident((1, D)), _resident((D, D)), _resident((D, D))],
        out_specs=[row_block, per_seq(WINDOW, KV_DIM), per_seq(WINDOW, KV_DIM), per_seq(CONV_W - 1, D)],
        out_shape=[jax.ShapeDtypeStruct((b, s, D), F32),
                   jax.ShapeDtypeStruct((b, WINDOW, KV_DIM), F32),
                   jax.ShapeDtypeStruct((b, WINDOW, KV_DIM), F32),
                   jax.ShapeDtypeStruct((b, CONV_W - 1, D), F32)],
        scratch_shapes=[pltpu.VMEM((N_KV, 2, WINDOW + tm, LANES), BF16),
                        pltpu.VMEM((N_KV, 2, WINDOW + tm, LANES), BF16),
                        pltpu.VMEM((tm, D), BF16),
                        pltpu.VMEM((tm, D), BF16),
                        pltpu.VMEM((HIST + tm, D), F32),
                        pltpu.VMEM((SUBLANES - 1, HIST + tm, D), F32),
                        pltpu.VMEM((tm, D), F32)],
        compiler_params=pltpu.CompilerParams(dimension_semantics=("arbitrary", "arbitrary"),
                                             vmem_limit_bytes=VMEM_LIMIT),
        name="prompt_stage_a",
    )(x, g1, w_in, sinks, w_o, cw, cb, lng, lnb, w_pb, w_out)


def _ffn_chunks(h, hn, prev_rows, wg_ref, wv_ref, fcw_ref, fcb_ref, wd_ref):
    def up_chunk(c):
        halves = []
        for part, w_ref in enumerate((wg_ref, wv_ref)):
            up = _dot(hn, w_ref[c])
            prev1, prev2 = prev_rows(part, c, up)
            w = fcw_ref[part, c]
            halves.append(w[0:1, :] * prev2 + w[1:2, :] * prev1 + w[2:3, :] * up + fcb_ref[part, c])
        return halves

    acc = h
    nxt = up_chunk(0)
    for c in range(FFN_NC):
        gate, val = nxt
        if c + 1 < FFN_NC:
            nxt = up_chunk(c + 1)
        act = (gate * _sigmoid(gate) * val).astype(BF16)
        acc = acc + _dot(act, wd_ref[c])
    return acc


def _prompt_ffn_body(h_ref, g2_ref, wg_ref, wv_ref, fcw_ref, fcb_ref, wd_ref, gf_ref,
                     y_ref, state_ref, carry_scr, *, tm):
    t = pl.program_id(1)

    @pl.when(t == 0)
    def _zero_history():
        carry_scr[...] = jnp.zeros(carry_scr.shape, F32)

    h = h_ref[0]
    hn = _rmsnorm(h, g2_ref[...]).astype(BF16)

    def prev_rows(part, c, up):
        ext = jnp.concatenate([carry_scr[part, c], up], axis=0)
        carry_scr[part, c] = up[tm - SUBLANES:, :]
        state_ref[0, part, c] = up[tm - (FFN_CONV_W - 1):, :]
        return ext[SUBLANES - 1:SUBLANES - 1 + tm, :], ext[SUBLANES - 2:SUBLANES - 2 + tm, :]

    y = _ffn_chunks(h, hn, prev_rows, wg_ref, wv_ref, fcw_ref, fcb_ref, wd_ref)
    y_ref[0] = _rmsnorm(y, gf_ref[...])


def _ffn_weight_specs():
    return [_resident((1, D)),
            _resident((FFN_NC, D, FFN_CH)), _resident((FFN_NC, D, FFN_CH)),
            _resident((2, FFN_NC, FFN_CONV_W, FFN_CH)), _resident((2, FFN_NC, 1, FFN_CH)),
            _resident((FFN_NC, FFN_CH, D)), _resident((1, D))]


def _prompt_ffn(h, ffn_w, *, tm):
    b, s, _ = h.shape
    row_block = pl.BlockSpec((1, tm, D), lambda i, t: (i, t, 0))
    state_shape = (b, 2, FFN_NC, FFN_CONV_W - 1, FFN_CH)
    return pl.pallas_call(
        functools.partial(_prompt_ffn_body, tm=tm),
        grid=(b, s // tm),
        in_specs=[row_block] + _ffn_weight_specs(),
        out_specs=[row_block,
                   pl.BlockSpec((1,) + state_shape[1:], lambda i, t: (i, 0, 0, 0, 0))],
        out_shape=[jax.ShapeDtypeStruct((b, s, D), F32), jax.ShapeDtypeStruct(state_shape, F32)],
        scratch_shapes=[pltpu.VMEM((2, FFN_NC, SUBLANES, FFN_CH), F32)],
        compiler_params=pltpu.CompilerParams(dimension_semantics=("arbitrary", "arbitrary"),
                                             vmem_limit_bytes=VMEM_LIMIT),
        name="prompt_ffn",
    )(h, *ffn_w)


def _sample_proj_body(x_ref, g1_ref, w_in_ref, q_ref, kv_ref, u_ref, gates_ref, *, n_seq, n_tok):
    x = jnp.concatenate([x_ref[:, i, :] for i in range(n_tok)], axis=0)
    xn = _rmsnorm(x, g1_ref[...]).astype(BF16)
    qkv = _dot(xn, w_in_ref[:, 0:QKV_W])
    q_ref[...] = qkv[:, 0:D]
    kv_ref[...] = qkv[:, D:QKV_W]
    glu = _dot(xn, w_in_ref[:, GLU_OFF:GATE_OFF])
    u_ref[...] = glu[:, 0:D] * _sigmoid(glu[:, D:2 * D])
    gates_ref[...] = _sigmoid(_dot(xn, w_in_ref[:, GATE_OFF:IN_DIM]))


def _sample_proj(x, g1, w_in, *, n_seq, n_tok):
    m = n_seq * n_tok
    full = lambda c: pl.BlockSpec((m, c), lambda i: (0, 0))
    return pl.pallas_call(
        functools.partial(_sample_proj_body, n_seq=n_seq, n_tok=n_tok),
        grid=(1,),
        in_specs=[pl.BlockSpec((n_seq, n_tok, D), lambda i: (0, 0, 0)), _resident((1, D)),
                  _resident((D, IN_DIM))],
        out_specs=[full(D), full(2 * KV_DIM), full(D), full(2 * D)],
        out_shape=[jax.ShapeDtypeStruct((m, D), F32), jax.ShapeDtypeStruct((m, 2 * KV_DIM), F32),
                   jax.ShapeDtypeStruct((m, D), F32), jax.ShapeDtypeStruct((m, 2 * D), F32)],
        compiler_params=pltpu.CompilerParams(dimension_semantics=("arbitrary",),
                                             vmem_limit_bytes=VMEM_LIMIT),
        name="sample_proj",
    )(x, g1, w_in)


def _sample_mix_body(q_ref, kv_ref, u_ref, kpast_ref, vpast_ref, cstate_ref, sink_ref,
                     cw_ref, cb_ref, lng_ref, lnb_ref,
                     attn_ref, cs_ref, kwin_ref, vwin_ref, cstate_out_ref,
                     kall_scr, vall_scr, qg_scr, o_scr, *, gs, n_tok):
    n_keys = 2 * WINDOW
    past_keep = WINDOW - n_tok

    @pl.when(pl.program_id(0) == 0)
    def _zero_padding():
        kall_scr[...] = jnp.zeros(kall_scr.shape, F32)
        vall_scr[...] = jnp.zeros(vall_scr.shape, F32)

    kwin_ref[:, 0:past_keep, :] = kpast_ref[:, n_tok:, :]
    vwin_ref[:, 0:past_keep, :] = vpast_ref[:, n_tok:, :]
    for i in range(n_tok):
        kwin_ref[:, past_keep + i, :] = kv_ref[i, :, 0:KV_DIM]
        vwin_ref[:, past_keep + i, :] = kv_ref[i, :, KV_DIM:2 * KV_DIM]

    qrow = lax.broadcasted_iota(jnp.int32, (N_HEADS // N_KV * n_tok, n_keys), 0) % n_tok
    key = lax.broadcasted_iota(jnp.int32, (N_HEADS // N_KV * n_tok, n_keys), 1)
    mask = jnp.logical_and(key > qrow, key <= qrow + WINDOW)[None]
    for g in range(N_KV):
        hsl = slice(g * HEAD_DIM, (g + 1) * HEAD_DIM)
        kall_scr[:, 0:WINDOW, :] = kpast_ref[:, :, hsl] * SCALE
        vall_scr[:, 0:WINDOW, :] = vpast_ref[:, :, hsl]
        for i in range(n_tok):
            kall_scr[:, WINDOW + i, :] = kv_ref[i, :, hsl] * SCALE
            vall_scr[:, WINDOW + i, :] = kv_ref[i, :, KV_DIM + g * HEAD_DIM:KV_DIM + (g + 1) * HEAD_DIM]
            for hh in range(N_HEADS // N_KV):
                h0 = (4 * g + hh) * HEAD_DIM
                qg_scr[:, hh * n_tok + i, :] = q_ref[i, :, h0:h0 + HEAD_DIM]
        qg = qg_scr[...].astype(BF16)
        s = jnp.einsum("nqd,nkd->nqk", qg, kall_scr[...].astype(BF16), preferred_element_type=F32)
        p = _softmax_with_sink(s, mask, sink_ref[g][None])
        o_scr[...] = jnp.einsum("nqk,nkd->nqd", p.astype(BF16), vall_scr[...].astype(BF16),
                                preferred_element_type=F32)
        for i in range(n_tok):
            for hh in range(N_HEADS // N_KV):
                h0 = (4 * g + hh) * HEAD_DIM
                attn_ref[i, :, h0:h0 + HEAD_DIM] = o_scr[:, hh * n_tok + i, :]

    n_state = CONV_W - 1

    def ext(r, sl):
        return cstate_ref[:, r, sl] if r < n_state else u_ref[r - n_state, :, sl]

    for r in range(n_state):
        src = r + n_tok
        cstate_out_ref[:, r, :] = ext(src, slice(None))

    lane_chunk = 256
    for i in range(n_tok):
        for lc in range(D // lane_chunk):
            sl = slice(lc * lane_chunk, (lc + 1) * lane_chunk)
            acc = jnp.broadcast_to(cb_ref[:, sl], (gs, lane_chunk))
            for j in range(CONV_W):
                acc = acc + ext(i + j, sl) * cw_ref[j:j + 1, sl]
            cs_ref[i, :, sl] = acc
    for i in range(n_tok):
        y = _layernorm(cs_ref[i], lng_ref[...], lnb_ref[...])
        cs_ref[i] = y * _sigmoid(y)


def _sample_mix(q, kv, u, kpast, vpast, cstate, sink_rows, cw, cb, lng, lnb, *, gs):
    n_tok, n_seq, _ = q.shape
    tok_block = lambda c: pl.BlockSpec((n_tok, gs, c), lambda s: (0, s, 0))
    seq_block = lambda r, c: pl.BlockSpec((gs, r, c), lambda s: (s, 0, 0))
    rows_per_group = N_HEADS // N_KV * n_tok
    return pl.pallas_call(
        functools.partial(_sample_mix_body, gs=gs, n_tok=n_tok),
        grid=(n_seq // gs,),
        in_specs=[tok_block(D), tok_block(2 * KV_DIM), tok_block(D),
                  seq_block(WINDOW, KV_DIM), seq_block(WINDOW, KV_DIM), seq_block(CONV_W - 1, D),
                  _resident((N_KV, rows_per_group, 1)),
                  _resident((CONV_W, D)), _resident((1, D)), _resident((1, D)), _resident((1, D))],
        out_specs=[tok_block(D), tok_block(D),
                   seq_block(WINDOW, KV_DIM), seq_block(WINDOW, KV_DIM), seq_block(CONV_W - 1, D)],
        out_shape=[jax.ShapeDtypeStruct((n_tok, n_seq, D), F32), jax.ShapeDtypeStruct((n_tok, n_seq, D), F32),
                   jax.ShapeDtypeStruct((n_seq, WINDOW, KV_DIM), F32),
                   jax.ShapeDtypeStruct((n_seq, WINDOW, KV_DIM), F32),
                   jax.ShapeDtypeStruct((n_seq, CONV_W - 1, D), F32)],
        scratch_shapes=[pltpu.VMEM((gs, 2 * WINDOW, HEAD_DIM), F32),
                        pltpu.VMEM((gs, 2 * WINDOW, HEAD_DIM), F32),
                        pltpu.VMEM((gs, rows_per_group, HEAD_DIM), F32),
                        pltpu.VMEM((gs, rows_per_group, HEAD_DIM), F32)],
        compiler_params=pltpu.CompilerParams(dimension_semantics=("arbitrary",),
                                             vmem_limit_bytes=VMEM_LIMIT),
        name="sample_mix",
    )(q, kv, u, kpast, vpast, cstate, sink_rows, cw, cb, lng, lnb)


def _sample_merge_body(x_ref, attn_ref, cs_ref, gates_ref, w_o_ref, w_pb_ref, w_out_ref, h_ref, *, n_tok):
    x = jnp.concatenate([x_ref[:, i, :] for i in range(n_tok)], axis=0)
    branch_a = _dot(attn_ref[...].astype(BF16), w_o_ref[...])
    branch_b = _dot(cs_ref[...].astype(BF16), w_pb_ref[...])
    merged = gates_ref[:, 0:D] * branch_a + gates_ref[:, D:2 * D] * branch_b
    h_ref[...] = x + _dot(merged.astype(BF16), w_out_ref[...])


def _sample_merge(x, attn, cs, gates, w_o, w_pb, w_out):
    n_seq, n_tok, _ = x.shape
    m = n_seq * n_tok
    full = lambda c: pl.BlockSpec((m, c), lambda i: (0, 0))
    return pl.pallas_call(
        functools.partial(_sample_merge_body, n_tok=n_tok),
        grid=(1,),
        in_specs=[pl.BlockSpec((n_seq, n_tok, D), lambda i: (0, 0, 0)), full(D), full(D), full(2 * D),
                  _resident((D, D)), _resident((D, D)), _resident((D, D))],
        out_specs=full(D),
        out_shape=jax.ShapeDtypeStruct((m, D), F32),
        compiler_params=pltpu.CompilerParams(dimension_semantics=("arbitrary",),
                                             vmem_limit_bytes=VMEM_LIMIT),
        name="sample_merge",
    )(x, attn, cs, gates, w_o, w_pb, w_out)


def _sample_ffn_body(h_ref, state_ref, g2_ref, wg_ref, wv_ref, fcw_ref, fcb_ref, wd_ref, gf_ref,
                     y_ref, state_out_ref, *, n_seq, n_tok):
    m = n_seq * n_tok
    h = h_ref[...]
    hn = _rmsnorm(h, g2_ref[...]).astype(BF16)

    def prev_rows(part, c, up):
        s0 = state_ref[0, part, c]
        s1 = state_ref[1, part, c]
        state_out_ref[0, part, c] = up[m - 2 * n_seq:m - n_seq, :]
        state_out_ref[1, part, c] = up[m - n_seq:, :]
        prev1 = jnp.concatenate([s1, up[:m - n_seq, :]], axis=0)
        prev2 = jnp.concatenate([s0, s1, up[:m - 2 * n_seq, :]], axis=0)
        return prev1, prev2

    y = _ffn_chunks(h, hn, prev_rows, wg_ref, wv_ref, fcw_ref, fcb_ref, wd_ref)
    y = _rmsnorm(y, gf_ref[...])
    for i in range(n_tok):
        y_ref[:, i, :] = y[i * n_seq:(i + 1) * n_seq, :]


def _sample_ffn(h, state, ffn_w, *, n_seq, n_tok):
    m = n_seq * n_tok
    state_shape = (FFN_CONV_W - 1, 2, FFN_NC, n_seq, FFN_CH)
    state_spec = pl.BlockSpec(state_shape, lambda i: (0, 0, 0, 0, 0))
    seq_major = pl.BlockSpec((n_seq, n_tok, D), lambda i: (0, 0, 0))
    return pl.pallas_call(
        functools.partial(_sample_ffn_body, n_seq=n_seq, n_tok=n_tok),
        grid=(1,),
        in_specs=[pl.BlockSpec((m, D), lambda i: (0, 0)), state_spec] + _ffn_weight_specs(),
        out_specs=[seq_major, state_spec],
        out_shape=[jax.ShapeDtypeStruct((n_seq, n_tok, D), F32), jax.ShapeDtypeStruct(state_shape, F32)],
        compiler_params=pltpu.CompilerParams(dimension_semantics=("arbitrary",),
                                             vmem_limit_bytes=VMEM_LIMIT),
        name="sample_ffn",
    )(h, state, *ffn_w)


def _layer(x_prompt, x_sample, k_past, v_past, conv_b_past, conv_f_past,
           norm1_g, w_in, sinks, w_o_attn, conv_b_w, conv_b_bias, ln_b_g, ln_b_b, w_pb, w_out,
           norm2_g, w_up, ffn_conv_w, ffn_conv_b, w_down, norm_f_g, *, tm_a, tm_ffn, gs):
    n_seq, n_tok, _ = x_sample.shape
    row = lambda v: v.reshape(1, -1)
    g1, g2, gf = row(norm1_g), row(norm2_g), row(norm_f_g)
    cb, lng, lnb = row(conv_b_bias), row(ln_b_g), row(ln_b_b)
    w_in_b, w_o_b, w_pb_b, w_out_b = (w.astype(BF16) for w in (w_in, w_o_attn, w_pb, w_out))
    w_up_c = w_up.astype(BF16).reshape(D, 2, FFN_NC, FFN_CH).transpose(1, 2, 0, 3)
    ffn_w = (g2, w_up_c[0], w_up_c[1],
             ffn_conv_w.reshape(FFN_CONV_W, 2, FFN_NC, FFN_CH).transpose(1, 2, 0, 3),
             ffn_conv_b.reshape(2, FFN_NC, 1, FFN_CH),
             w_down.astype(BF16).reshape(FFN_NC, FFN_CH, D), gf)

    h_p, k_win_p, v_win_p, conv_b_p = _prompt_stage_a(
        x_prompt, g1, w_in_b, sinks, w_o_b, jnp.broadcast_to(conv_b_w[:, None, :], (CONV_W, SUBLANES, D)),
        cb, lng, lnb, w_pb_b, w_out_b, tm=tm_a)
    y_p, ffn_state_p = _prompt_ffn(h_p, ffn_w, tm=tm_ffn)
    b = x_prompt.shape[0]
    conv_f_p = ffn_state_p.transpose(0, 3, 1, 2, 4).reshape(b, FFN_CONV_W - 1, 2 * D_FF)

    q, kv, u, gates = _sample_proj(x_sample, g1, w_in_b, n_seq=n_seq, n_tok=n_tok)
    tok_major = lambda a: a.reshape(n_tok, n_seq, a.shape[-1])
    sink_rows = jnp.repeat(sinks.reshape(N_KV, N_HEADS // N_KV), n_tok, axis=1)[:, :, None]
    attn, cs, k_win_s, v_win_s, conv_b_s = _sample_mix(
        tok_major(q), tok_major(kv), tok_major(u),
        k_past.reshape(n_seq, WINDOW, KV_DIM), v_past.reshape(n_seq, WINDOW, KV_DIM), conv_b_past,
        sink_rows, conv_b_w, cb, lng, lnb, gs=gs)
    ffn_state = conv_f_past.reshape(n_seq, FFN_CONV_W - 1, 2, FFN_NC, FFN_CH).transpose(1, 2, 3, 0, 4)
    h_s = _sample_merge(x_sample, attn.reshape(n_seq * n_tok, D), cs.reshape(n_seq * n_tok, D), gates,
                        w_o_b, w_pb_b, w_out_b)
    y_s, ffn_state_s = _sample_ffn(h_s, ffn_state, ffn_w, n_seq=n_seq, n_tok=n_tok)
    conv_f_s = ffn_state_s.transpose(3, 0, 1, 2, 4).reshape(n_seq, FFN_CONV_W - 1, 2 * D_FF)

    kv_shape = lambda a: a.reshape(a.shape[0], WINDOW, N_KV, HEAD_DIM)
    return (y_p, y_s,
            kv_shape(k_win_p), kv_shape(v_win_p), conv_b_p, conv_f_p,
            kv_shape(k_win_s), kv_shape(v_win_s), conv_b_s, conv_f_s)


def kernel(x_prompt, x_sample, cache_k_win, cache_v_win, state_conv_b, state_conv_ffn, norm1_g, w_in, sinks,
           w_o_attn, conv_b_w, conv_b_bias, ln_b_g, ln_b_b, w_pb, w_out, norm2_g, w_up, ffn_conv_w,
           ffn_conv_b, w_down, norm_f_g):
    assert w_in.shape[0] == 1, "single-layer stack only"
    res = _layer(x_prompt, x_sample, cache_k_win[0], cache_v_win[0], state_conv_b[0], state_conv_ffn[0],
                 norm1_g[0], w_in[0], sinks[0], w_o_attn[0], conv_b_w[0], conv_b_bias[0], ln_b_g[0],
                 ln_b_b[0], w_pb[0], w_out[0], norm2_g[0], w_up[0], ffn_conv_w[0], ffn_conv_b[0],
                 w_down[0], norm_f_g, tm_a=256, tm_ffn=512, gs=16)
    return res[:2] + tuple(r[None] for r in res[2:])
```

```python
import functools

import jax
import jax.numpy as jnp
from jax import lax
from jax.experimental import pallas as pl
from jax.experimental.pallas import tpu as pltpu

F32 = jnp.float32
BF16 = jnp.bfloat16

D = 1024
N_HEADS, N_KV, HEAD_DIM = 16, 4, 64
WINDOW = 128
KV_DIM = N_KV * HEAD_DIM
CONV_W = 31
D_FF = 2816
FFN_CONV_W = 3
EPS = 1e-6
NEG_INF = -1e30
SCALE = HEAD_DIM ** -0.5
QKV_W = D + 2 * KV_DIM
GLU_OFF = QKV_W
GATE_OFF = QKV_W + 2 * D
IN_DIM = QKV_W + 4 * D
LANES = 128
SUBLANES = 8
GROUP = N_HEADS // N_KV
ATTN_BATCH = 4
FFN_CH = 256
FFN_NC = D_FF // FFN_CH
VMEM_LIMIT = 56 * 1024 * 1024


def _resident(shape):
    nd = len(shape)
    return pl.BlockSpec(shape, lambda *_: (0,) * nd, pipeline_mode=pl.Buffered(1))


def _sigmoid(x):
    return 1.0 / (1.0 + jnp.exp(-x))


def _rmsnorm(x, g):
    return x * lax.rsqrt(jnp.mean(x * x, axis=-1, keepdims=True) + EPS) * g


def _layernorm(x, g, b):
    mu = jnp.mean(x, axis=-1, keepdims=True)
    xc = x - mu
    var = jnp.mean(xc * xc, axis=-1, keepdims=True)
    return xc * lax.rsqrt(var + EPS) * g + b


def _dot(a, b):
    return jnp.dot(a, b, preferred_element_type=F32)


def _softmax_with_sink(s, mask, sink):
    s = jnp.where(mask, s, NEG_INF)
    m = jnp.maximum(jnp.max(s, axis=-1, keepdims=True), sink)
    e = jnp.exp(s - m)
    den = jnp.sum(e, axis=-1, keepdims=True) + jnp.exp(sink - m)
    return e / den


def _split_heads_block_diag(x, lo):
    xr = pltpu.roll(x, HEAD_DIM, axis=1)
    z = jnp.zeros_like(x)
    even = (jnp.where(lo, x, z), jnp.where(lo, z, xr))
    odd = (jnp.where(lo, xr, z), jnp.where(lo, z, x))
    return even, odd


def _prompt_a_body(x_ref, g1_ref, w_in_ref, sinks_ref, w_o_ref, cw_ref, cb_ref, lng_ref, lnb_ref,
                   w_pb_ref, w_out_ref,
                   h_ref, kwin_ref, vwin_ref, cbs_ref,
                   kbd_scr, vbd_scr, q_scr, attn_scr, xn_scr, ext_scr, carry_scr, c_scr, *, tm):
    t = pl.program_id(1)
    last = pl.num_programs(1) - 1

    @pl.when(t == 0)
    def _zero_history():
        kbd_scr[:, :, 0:WINDOW, :] = jnp.zeros((N_KV, 2, WINDOW, LANES), BF16)
        vbd_scr[:, :, 0:WINDOW, :] = jnp.zeros((N_KV, 2, WINDOW, LANES), BF16)
        carry_scr[...] = jnp.zeros(carry_scr.shape, F32)

    x = x_ref[0]
    xn_f32 = _rmsnorm(x, g1_ref[...])
    xn = xn_f32.astype(BF16)

    qkv = _dot(xn, w_in_ref[:, 0:QKV_W])
    q_scr[...] = qkv[:, 0:D].astype(BF16)
    k_new = qkv[:, D:D + KV_DIM]
    v_new = qkv[:, D + KV_DIM:QKV_W]

    @pl.when(t == last)
    def _emit_kv_window():
        kwin_ref[0] = k_new[tm - WINDOW:, :]
        vwin_ref[0] = v_new[tm - WINDOW:, :]

    lo = lax.broadcasted_iota(jnp.int32, (tm, LANES), 1) < HEAD_DIM
    for m in range(N_KV // 2):
        k_even, k_odd = _split_heads_block_diag(k_new[:, LANES * m:LANES * (m + 1)] * SCALE, lo)
        v_even, v_odd = _split_heads_block_diag(v_new[:, LANES * m:LANES * (m + 1)], lo)
        for half in range(2):
            kbd_scr[2 * m, half, WINDOW:, :] = k_even[half].astype(BF16)
            kbd_scr[2 * m + 1, half, WINDOW:, :] = k_odd[half].astype(BF16)
            vbd_scr[2 * m, half, WINDOW:, :] = v_even[half].astype(BF16)
            vbd_scr[2 * m + 1, half, WINDOW:, :] = v_odd[half].astype(BF16)

    row = lax.broadcasted_iota(jnp.int32, (WINDOW, 2 * WINDOW), 0)
    col = lax.broadcasted_iota(jnp.int32, (WINDOW, 2 * WINDOW), 1)
    band = jnp.logical_and(col > row, col <= row + WINDOW)
    first_mask = jnp.logical_and(band, col >= jnp.where(t > 0, 0, WINDOW))

    def scores(qb, g):
        r0 = qb * WINDOW
        c0 = 2 * LANES * g
        q_pairs = jnp.concatenate([q_scr[r0:r0 + WINDOW, c0:c0 + LANES],
                                   q_scr[r0:r0 + WINDOW, c0 + LANES:c0 + 2 * LANES]], axis=0)
        kbd = jnp.concatenate([kbd_scr[g, 0, r0:r0 + 2 * WINDOW, :],
                               kbd_scr[g, 1, r0:r0 + 2 * WINDOW, :]], axis=0)
        return lax.dot_general(q_pairs, kbd, (((1,), (1,)), ((), ())), preferred_element_type=F32)

    def attend(item, s):
        qb, g = item
        r0 = qb * WINDOW
        c0 = 2 * LANES * g
        mask = first_mask if qb == 0 else band
        probs = []
        for pair in range(2):
            halves = []
            for hh in range(2):
                sink = sinks_ref[GROUP * g + 2 * pair + hh]
                p = _softmax_with_sink(s[pair * WINDOW:(pair + 1) * WINDOW, 2 * WINDOW * hh:2 * WINDOW * (hh + 1)],
                                       mask, sink)
                halves.append(p.astype(BF16))
            probs.append(jnp.concatenate(halves, axis=1))
        vbd = jnp.concatenate([vbd_scr[g, 0, r0:r0 + 2 * WINDOW, :],
                               vbd_scr[g, 1, r0:r0 + 2 * WINDOW, :]], axis=0)
        o = _dot(jnp.concatenate(probs, axis=0), vbd)
        attn_scr[r0:r0 + WINDOW, c0:c0 + LANES] = o[0:WINDOW, :].astype(BF16)
        attn_scr[r0:r0 + WINDOW, c0 + LANES:c0 + 2 * LANES] = o[WINDOW:2 * WINDOW, :].astype(BF16)

    items = [(qb, g) for qb in range(tm // WINDOW) for g in range(N_KV)]
    batches = [items[i:i + ATTN_BATCH] for i in range(0, len(items), ATTN_BATCH)]
    s_next = [scores(*it) for it in batches[0]]
    for k, batch in enumerate(batches):
        s_cur = s_next
        if k + 1 < len(batches):
            s_next = [scores(*it) for it in batches[k + 1]]
        for it, s in zip(batch, s_cur):
            attend(it, s)

    kbd_scr[:, :, 0:WINDOW, :] = kbd_scr[:, :, tm:tm + WINDOW, :]
    vbd_scr[:, :, 0:WINDOW, :] = vbd_scr[:, :, tm:tm + WINDOW, :]
    branch_a = _dot(attn_scr[...], w_o_ref[...])

    n_tiles = tm // SUBLANES
    assert n_tiles >= CONV_W - 1, "a tap must not reach further back than the previous-sublane tiles"
    xn_scr[...] = xn_f32.reshape(SUBLANES, n_tiles, D)
    xn_perm = jnp.concatenate([xn_scr[:, p, :] for p in range(n_tiles)], axis=0).astype(BF16)
    glu = _dot(xn_perm, w_in_ref[:, GLU_OFF:GATE_OFF])
    u_perm = (glu[:, 0:D] * _sigmoid(glu[:, D:2 * D])).reshape(n_tiles, SUBLANES, D)
    ext_scr[n_tiles:2 * n_tiles] = u_perm
    prev_last = jnp.broadcast_to(carry_scr[...][:, None, :], (n_tiles, SUBLANES, D))
    first_sub = lax.broadcasted_iota(jnp.int32, (n_tiles, SUBLANES, D), 1) == 0
    ext_scr[0:n_tiles] = jnp.where(first_sub, prev_last, pltpu.roll(u_perm, 1, axis=1))

    @pl.when(t == last)
    def _emit_conv_state():
        cbs_ref[0] = ext_scr[2 * n_tiles - (CONV_W - 1):2 * n_tiles, SUBLANES - 1, :]

    carry_scr[...] = ext_scr[n_tiles:2 * n_tiles, SUBLANES - 1, :]

    tiles = 8
    lane_chunk = 128

    def _conv_rows(i, carry):
        p0 = i * tiles
        for lc in range(D // lane_chunk):
            sl = slice(lc * lane_chunk, (lc + 1) * lane_chunk)
            acc = jnp.broadcast_to(cb_ref[:, sl], (tiles * SUBLANES, lane_chunk))
            for j in range(CONV_W):
                back = CONV_W - 1 - j
                w = jnp.concatenate([cw_ref[j, :, sl]] * tiles, axis=0)
                win = ext_scr[pl.ds(n_tiles + p0 - back, tiles), :, sl]
                acc = acc + win.reshape(tiles * SUBLANES, lane_chunk) * w
            c_scr[pl.ds(p0, tiles), :, sl] = acc.reshape(tiles, SUBLANES, lane_chunk)
        return carry

    lax.fori_loop(0, n_tiles // tiles, _conv_rows, 0)

    y = _layernorm(c_scr[...].reshape(tm, D), lng_ref[...], lnb_ref[...])
    c_scr[...] = _dot((y * _sigmoid(y)).astype(BF16), w_pb_ref[...]).reshape(n_tiles, SUBLANES, D)
    branch_b = jnp.concatenate(
        [c_scr[(n * SUBLANES) % n_tiles:(n * SUBLANES) % n_tiles + SUBLANES, (n * SUBLANES) // n_tiles, :]
         for n in range(n_tiles)], axis=0)

    gates = _dot(xn, w_in_ref[:, GATE_OFF:IN_DIM])
    merged = _sigmoid(gates[:, 0:D]) * branch_a + _sigmoid(gates[:, D:2 * D]) * branch_b
    h_ref[0] = x + _dot(merged.astype(BF16), w_out_ref[...])


def _prompt_stage_a(x, g1, w_in, sinks, w_o, cw, cb, lng, lnb, w_pb, w_out, *, tm):
    b, s, _ = x.shape
    grid = (b, s // tm)
    row_block = pl.BlockSpec((1, tm, D), lambda i, t: (i, t, 0))
    per_seq = lambda r, c: pl.BlockSpec((1, r, c), lambda i, t: (i, 0, 0))
    return pl.pallas_call(
        functools.partial(_prompt_a_body, tm=tm),
        grid=grid,
        in_specs=[row_block, _resident((1, D)), _resident((D, IN_DIM)),
                  pl.BlockSpec(memory_space=pltpu.SMEM),
                  _resident((D, D)), _resident((CONV_W, SUBLANES, D)), _resident((1, D)), _resident((1, D)),
                  _resident((1, D)), _resident((D, D)), _resident((D, D))],
        out_specs=[row_block, per_seq(WINDOW, KV_DIM), per_seq(WINDOW, KV_DIM), per_seq(CONV_W - 1, D)],
        out_shape=[jax.ShapeDtypeStruct((b, s, D), F32),
                   jax.ShapeDtypeStruct((b, WINDOW, KV_DIM), F32),
                   jax.ShapeDtypeStruct((b, WINDOW, KV_DIM), F32),
                   jax.ShapeDtypeStruct((b, CONV_W - 1, D), F32)],
        scratch_shapes=[pltpu.VMEM((N_KV, 2, WINDOW + tm, LANES), BF16),
                        pltpu.VMEM((N_KV, 2, WINDOW + tm, LANES), BF16),
                        pltpu.VMEM((tm, D), BF16),
                        pltpu.VMEM((tm, D), BF16),
                        pltpu.VMEM((SUBLANES, tm // SUBLANES, D), F32),
                        pltpu.VMEM((2 * tm // SUBLANES, SUBLANES, D), F32),
                        pltpu.VMEM((tm // SUBLANES, D), F32),
                        pltpu.VMEM((tm // SUBLANES, SUBLANES, D), F32)],
        compiler_params=pltpu.CompilerParams(dimension_semantics=("arbitrary", "arbitrary"),
                                             vmem_limit_bytes=VMEM_LIMIT),
        name="prompt_stage_a",
    )(x, g1, w_in, sinks, w_o, cw, cb, lng, lnb, w_pb, w_out)


def _ffn_chunks(h, hn, prev_rows, wg_ref, wv_ref, fcw_ref, fcb_ref, wd_ref):
    def up_chunk(c):
        halves = []
        for part, w_ref in enumerate((wg_ref, wv_ref)):
            up = _dot(hn, w_ref[c])
            prev1, prev2 = prev_rows(part, c, up)
            w = fcw_ref[part, c]
            halves.append(w[0:1, :] * prev2 + w[1:2, :] * prev1 + w[2:3, :] * up + fcb_ref[part, c])
        return halves

    acc = h
    nxt = up_chunk(0)
    for c in range(FFN_NC):
        gate, val = nxt
        if c + 1 < FFN_NC:
            nxt = up_chunk(c + 1)
        act = (gate * _sigmoid(gate) * val).astype(BF16)
        acc = acc + _dot(act, wd_ref[c])
    return acc


def _prompt_ffn_body(h_ref, g2_ref, wg_ref, wv_ref, fcw_ref, fcb_ref, wd_ref, gf_ref,
                     y_ref, state_ref, carry_scr, *, tm):
    t = pl.program_id(1)

    @pl.when(t == 0)
    def _zero_history():
        carry_scr[...] = jnp.zeros(carry_scr.shape, F32)

    h = h_ref[0]
    hn = _rmsnorm(h, g2_ref[...]).astype(BF16)

    def prev_rows(part, c, up):
        ext = jnp.concatenate([carry_scr[part, c], up], axis=0)
        carry_scr[part, c] = up[tm - SUBLANES:, :]
        state_ref[0, part, c] = up[tm - (FFN_CONV_W - 1):, :]
        return ext[SUBLANES - 1:SUBLANES - 1 + tm, :], ext[SUBLANES - 2:SUBLANES - 2 + tm, :]

    y = _ffn_chunks(h, hn, prev_rows, wg_ref, wv_ref, fcw_ref, fcb_ref, wd_ref)
    y_ref[0] = _rmsnorm(y, gf_ref[...])


def _ffn_weight_specs():
    return [_resident((1, D)),
            _resident((FFN_NC, D, FFN_CH)), _resident((FFN_NC, D, FFN_CH)),
            _resident((2, FFN_NC, FFN_CONV_W, FFN_CH)), _resident((2, FFN_NC, 1, FFN_CH)),
            _resident((FFN_NC, FFN_CH, D)), _resident((1, D))]


def _prompt_ffn(h, ffn_w, *, tm):
    b, s, _ = h.shape
    row_block = pl.BlockSpec((1, tm, D), lambda i, t: (i, t, 0))
    state_shape = (b, 2, FFN_NC, FFN_CONV_W - 1, FFN_CH)
    return pl.pallas_call(
        functools.partial(_prompt_ffn_body, tm=tm),
        grid=(b, s // tm),
        in_specs=[row_block] + _ffn_weight_specs(),
        out_specs=[row_block,
                   pl.BlockSpec((1,) + state_shape[1:], lambda i, t: (i, 0, 0, 0, 0))],
        out_shape=[jax.ShapeDtypeStruct((b, s, D), F32), jax.ShapeDtypeStruct(state_shape, F32)],
        scratch_shapes=[pltpu.VMEM((2, FFN_NC, SUBLANES, FFN_CH), F32)],
        compiler_params=pltpu.CompilerParams(dimension_semantics=("arbitrary", "arbitrary"),
                                             vmem_limit_bytes=VMEM_LIMIT),
        name="prompt_ffn",
    )(h, *ffn_w)


def _sample_proj_body(x_ref, g1_ref, w_in_ref, q_ref, kv_ref, u_ref, gates_ref, *, n_seq, n_tok):
    x = jnp.concatenate([x_ref[:, i, :] for i in range(n_tok)], axis=0)
    xn = _rmsnorm(x, g1_ref[...]).astype(BF16)
    qkv = _dot(xn, w_in_ref[:, 0:QKV_W])
    q_ref[...] = qkv[:, 0:D]
    kv_ref[...] = qkv[:, D:QKV_W]
    glu = _dot(xn, w_in_ref[:, GLU_OFF:GATE_OFF])
    u_ref[...] = glu[:, 0:D] * _sigmoid(glu[:, D:2 * D])
    gates_ref[...] = _sigmoid(_dot(xn, w_in_ref[:, GATE_OFF:IN_DIM]))


def _sample_proj(x, g1, w_in, *, n_seq, n_tok):
    m = n_seq * n_tok
    full = lambda c: pl.BlockSpec((m, c), lambda i: (0, 0))
    return pl.pallas_call(
        functools.partial(_sample_proj_body, n_seq=n_seq, n_tok=n_tok),
        grid=(1,),
        in_specs=[pl.BlockSpec((n_seq, n_tok, D), lambda i: (0, 0, 0)), _resident((1, D)),
                  _resident((D, IN_DIM))],
        out_specs=[full(D), full(2 * KV_DIM), full(D), full(2 * D)],
        out_shape=[jax.ShapeDtypeStruct((m, D), F32), jax.ShapeDtypeStruct((m, 2 * KV_DIM), F32),
                   jax.ShapeDtypeStruct((m, D), F32), jax.ShapeDtypeStruct((m, 2 * D), F32)],
        compiler_params=pltpu.CompilerParams(dimension_semantics=("arbitrary",),
                                             vmem_limit_bytes=VMEM_LIMIT),
        name="sample_proj",
    )(x, g1, w_in)


def _sample_mix_body(q_ref, kv_ref, u_ref, kpast_ref, vpast_ref, cstate_ref, sink_ref,
                     cw_ref, cb_ref, lng_ref, lnb_ref,
                     attn_ref, cs_ref, kwin_ref, vwin_ref, cstate_out_ref,
                     kall_scr, vall_scr, qg_scr, o_scr, *, gs, n_tok):
    n_keys = 2 * WINDOW
    past_keep = WINDOW - n_tok

    @pl.when(pl.program_id(0) == 0)
    def _zero_padding():
        kall_scr[...] = jnp.zeros(kall_scr.shape, F32)
        vall_scr[...] = jnp.zeros(vall_scr.shape, F32)

    kwin_ref[:, 0:past_keep, :] = kpast_ref[:, n_tok:, :]
    vwin_ref[:, 0:past_keep, :] = vpast_ref[:, n_tok:, :]
    for i in range(n_tok):
        kwin_ref[:, past_keep + i, :] = kv_ref[i, :, 0:KV_DIM]
        vwin_ref[:, past_keep + i, :] = kv_ref[i, :, KV_DIM:2 * KV_DIM]

    qrow = lax.broadcasted_iota(jnp.int32, (N_HEADS // N_KV * n_tok, n_keys), 0) % n_tok
    key = lax.broadcasted_iota(jnp.int32, (N_HEADS // N_KV * n_tok, n_keys), 1)
    mask = jnp.logical_and(key > qrow, key <= qrow + WINDOW)[None]
    for g in range(N_KV):
        hsl = slice(g * HEAD_DIM, (g + 1) * HEAD_DIM)
        kall_scr[:, 0:WINDOW, :] = kpast_ref[:, :, hsl] * SCALE
        vall_scr[:, 0:WINDOW, :] = vpast_ref[:, :, hsl]
        for i in range(n_tok):
            kall_scr[:, WINDOW + i, :] = kv_ref[i, :, hsl] * SCALE
            vall_scr[:, WINDOW + i, :] = kv_ref[i, :, KV_DIM + g * HEAD_DIM:KV_DIM + (g + 1) * HEAD_DIM]
            for hh in range(N_HEADS // N_KV):
                h0 = (4 * g + hh) * HEAD_DIM
                qg_scr[:, hh * n_tok + i, :] = q_ref[i, :, h0:h0 + HEAD_DIM]
        qg = qg_scr[...].astype(BF16)
        s = jnp.einsum("nqd,nkd->nqk", qg, kall_scr[...].astype(BF16), preferred_element_type=F32)
        p = _softmax_with_sink(s, mask, sink_ref[g][None])
        o_scr[...] = jnp.einsum("nqk,nkd->nqd", p.astype(BF16), vall_scr[...].astype(BF16),
                                preferred_element_type=F32)
        for i in range(n_tok):
            for hh in range(N_HEADS // N_KV):
                h0 = (4 * g + hh) * HEAD_DIM
                attn_ref[i, :, h0:h0 + HEAD_DIM] = o_scr[:, hh * n_tok + i, :]

    n_state = CONV_W - 1

    def ext(r, sl):
        return cstate_ref[:, r, sl] if r < n_state else u_ref[r - n_state, :, sl]

    for r in range(n_state):
        src = r + n_tok
        cstate_out_ref[:, r, :] = ext(src, slice(None))

    lane_chunk = 256
    for i in range(n_tok):
        for lc in range(D // lane_chunk):
            sl = slice(lc * lane_chunk, (lc + 1) * lane_chunk)
            acc = jnp.broadcast_to(cb_ref[:, sl], (gs, lane_chunk))
            for j in range(CONV_W):
                acc = acc + ext(i + j, sl) * cw_ref[j:j + 1, sl]
            cs_ref[i, :, sl] = acc
    for i in range(n_tok):
        y = _layernorm(cs_ref[i], lng_ref[...], lnb_ref[...])
        cs_ref[i] = y * _sigmoid(y)


def _sample_mix(q, kv, u, kpast, vpast, cstate, sink_rows, cw, cb, lng, lnb, *, gs):
    n_tok, n_seq, _ = q.shape
    tok_block = lambda c: pl.BlockSpec((n_tok, gs, c), lambda s: (0, s, 0))
    seq_block = lambda r, c: pl.BlockSpec((gs, r, c), lambda s: (s, 0, 0))
    rows_per_group = N_HEADS // N_KV * n_tok
    return pl.pallas_call(
        functools.partial(_sample_mix_body, gs=gs, n_tok=n_tok),
        grid=(n_seq // gs,),
        in_specs=[tok_block(D), tok_block(2 * KV_DIM), tok_block(D),
                  seq_block(WINDOW, KV_DIM), seq_block(WINDOW, KV_DIM), seq_block(CONV_W - 1, D),
                  _resident((N_KV, rows_per_group, 1)),
                  _resident((CONV_W, D)), _resident((1, D)), _resident((1, D)), _resident((1, D))],
        out_specs=[tok_block(D), tok_block(D),
                   seq_block(WINDOW, KV_DIM), seq_block(WINDOW, KV_DIM), seq_block(CONV_W - 1, D)],
        out_shape=[jax.ShapeDtypeStruct((n_tok, n_seq, D), F32), jax.ShapeDtypeStruct((n_tok, n_seq, D), F32),
                   jax.ShapeDtypeStruct((n_seq, WINDOW, KV_DIM), F32),
                   jax.ShapeDtypeStruct((n_seq, WINDOW, KV_DIM), F32),
                   jax.ShapeDtypeStruct((n_seq, CONV_W - 1, D), F32)],
        scratch_shapes=[pltpu.VMEM((gs, 2 * WINDOW, HEAD_DIM), F32),
                        pltpu.VMEM((gs, 2 * WINDOW, HEAD_DIM), F32),
                        pltpu.VMEM((gs, rows_per_group, HEAD_DIM), F32),
                        pltpu.VMEM((gs, rows_per_group, HEAD_DIM), F32)],
        compiler_params=pltpu.CompilerParams(dimension_semantics=("arbitrary",),
                                             vmem_limit_bytes=VMEM_LIMIT),
        name="sample_mix",
    )(q, kv, u, kpast, vpast, cstate, sink_rows, cw, cb, lng, lnb)


def _sample_merge_body(x_ref, attn_ref, cs_ref, gates_ref, w_o_ref, w_pb_ref, w_out_ref, h_ref, *, n_tok):
    x = jnp.concatenate([x_ref[:, i, :] for i in range(n_tok)], axis=0)
    branch_a = _dot(attn_ref[...].astype(BF16), w_o_ref[...])
    branch_b = _dot(cs_ref[...].astype(BF16), w_pb_ref[...])
    merged = gates_ref[:, 0:D] * branch_a + gates_ref[:, D:2 * D] * branch_b
    h_ref[...] = x + _dot(merged.astype(BF16), w_out_ref[...])


def _sample_merge(x, attn, cs, gates, w_o, w_pb, w_out):
    n_seq, n_tok, _ = x.shape
    m = n_seq * n_tok
    full = lambda c: pl.BlockSpec((m, c), lambda i: (0, 0))
    return pl.pallas_call(
        functools.partial(_sample_merge_body, n_tok=n_tok),
        grid=(1,),
        in_specs=[pl.BlockSpec((n_seq, n_tok, D), lambda i: (0, 0, 0)), full(D), full(D), full(2 * D),
                  _resident((D, D)), _resident((D, D)), _resident((D, D))],
        out_specs=full(D),
        out_shape=jax.ShapeDtypeStruct((m, D), F32),
        compiler_params=pltpu.CompilerParams(dimension_semantics=("arbitrary",),
                                             vmem_limit_bytes=VMEM_LIMIT),
        name="sample_merge",
    )(x, attn, cs, gates, w_o, w_pb, w_out)


def _sample_ffn_body(h_ref, state_ref, g2_ref, wg_ref, wv_ref, fcw_ref, fcb_ref, wd_ref, gf_ref,
                     y_ref, state_out_ref, *, n_seq, n_tok):
    m = n_seq * n_tok
    h = h_ref[...]
    hn = _rmsnorm(h, g2_ref[...]).astype(BF16)

    def prev_rows(part, c, up):
        s0 = state_ref[0, part, c]
        s1 = state_ref[1, part, c]
        state_out_ref[0, part, c] = up[m - 2 * n_seq:m - n_seq, :]
        state_out_ref[1, part, c] = up[m - n_seq:, :]
        prev1 = jnp.concatenate([s1, up[:m - n_seq, :]], axis=0)
        prev2 = jnp.concatenate([s0, s1, up[:m - 2 * n_seq, :]], axis=0)
        return prev1, prev2

    y = _ffn_chunks(h, hn, prev_rows, wg_ref, wv_ref, fcw_ref, fcb_ref, wd_ref)
    y = _rmsnorm(y, gf_ref[...])
    for i in range(n_tok):
        y_ref[:, i, :] = y[i * n_seq:(i + 1) * n_seq, :]


def _sample_ffn(h, state, ffn_w, *, n_seq, n_tok):
    m = n_seq * n_tok
    state_shape = (FFN_CONV_W - 1, 2, FFN_NC, n_seq, FFN_CH)
    state_spec = pl.BlockSpec(state_shape, lambda i: (0, 0, 0, 0, 0))
    seq_major = pl.BlockSpec((n_seq, n_tok, D), lambda i: (0, 0, 0))
    return pl.pallas_call(
        functools.partial(_sample_ffn_body, n_seq=n_seq, n_tok=n_tok),
        grid=(1,),
        in_specs=[pl.BlockSpec((m, D), lambda i: (0, 0)), state_spec] + _ffn_weight_specs(),
        out_specs=[seq_major, state_spec],
        out_shape=[jax.ShapeDtypeStruct((n_seq, n_tok, D), F32), jax.ShapeDtypeStruct(state_shape, F32)],
        compiler_params=pltpu.CompilerParams(dimension_semantics=("arbitrary",),
                                             vmem_limit_bytes=VMEM_LIMIT),
        name="sample_ffn",
    )(h, state, *ffn_w)


def _layer(x_prompt, x_sample, k_past, v_past, conv_b_past, conv_f_past,
           norm1_g, w_in, sinks, w_o_attn, conv_b_w, conv_b_bias, ln_b_g, ln_b_b, w_pb, w_out,
           norm2_g, w_up, ffn_conv_w, ffn_conv_b, w_down, norm_f_g, *, tm_a, tm_ffn, gs):
    n_seq, n_tok, _ = x_sample.shape
    row = lambda v: v.reshape(1, -1)
    g1, g2, gf = row(norm1_g), row(norm2_g), row(norm_f_g)
    cb, lng, lnb = row(conv_b_bias), row(ln_b_g), row(ln_b_b)
    w_in_b, w_o_b, w_pb_b, w_out_b = (w.astype(BF16) for w in (w_in, w_o_attn, w_pb, w_out))
    w_up_c = w_up.astype(BF16).reshape(D, 2, FFN_NC, FFN_CH).transpose(1, 2, 0, 3)
    ffn_w = (g2, w_up_c[0], w_up_c[1],
             ffn_conv_w.reshape(FFN_CONV_W, 2, FFN_NC, FFN_CH).transpose(1, 2, 0, 3),
             ffn_conv_b.reshape(2, FFN_NC, 1, FFN_CH),
             w_down.astype(BF16).reshape(FFN_NC, FFN_CH, D), gf)

    h_p, k_win_p, v_win_p, conv_b_p = _prompt_stage_a(
        x_prompt, g1, w_in_b, sinks, w_o_b, jnp.broadcast_to(conv_b_w[:, None, :], (CONV_W, SUBLANES, D)),
        cb, lng, lnb, w_pb_b, w_out_b, tm=tm_a)
    y_p, ffn_state_p = _prompt_ffn(h_p, ffn_w, tm=tm_ffn)
    b = x_prompt.shape[0]
    conv_f_p = ffn_state_p.transpose(0, 3, 1, 2, 4).reshape(b, FFN_CONV_W - 1, 2 * D_FF)

    q, kv, u, gates = _sample_proj(x_sample, g1, w_in_b, n_seq=n_seq, n_tok=n_tok)
    tok_major = lambda a: a.reshape(n_tok, n_seq, a.shape[-1])
    sink_rows = jnp.repeat(sinks.reshape(N_KV, N_HEADS // N_KV), n_tok, axis=1)[:, :, None]
    attn, cs, k_win_s, v_win_s, conv_b_s = _sample_mix(
        tok_major(q), tok_major(kv), tok_major(u),
        k_past.reshape(n_seq, WINDOW, KV_DIM), v_past.reshape(n_seq, WINDOW, KV_DIM), conv_b_past,
        sink_rows, conv_b_w, cb, lng, lnb, gs=gs)
    ffn_state = conv_f_past.reshape(n_seq, FFN_CONV_W - 1, 2, FFN_NC, FFN_CH).transpose(1, 2, 3, 0, 4)
    h_s = _sample_merge(x_sample, attn.reshape(n_seq * n_tok, D), cs.reshape(n_seq * n_tok, D), gates,
                        w_o_b, w_pb_b, w_out_b)
    y_s, ffn_state_s = _sample_ffn(h_s, ffn_state, ffn_w, n_seq=n_seq, n_tok=n_tok)
    conv_f_s = ffn_state_s.transpose(3, 0, 1, 2, 4).reshape(n_seq, FFN_CONV_W - 1, 2 * D_FF)

    kv_shape = lambda a: a.reshape(a.shape[0], WINDOW, N_KV, HEAD_DIM)
    return (y_p, y_s,
            kv_shape(k_win_p), kv_shape(v_win_p), conv_b_p, conv_f_p,
            kv_shape(k_win_s), kv_shape(v_win_s), conv_b_s, conv_f_s)


def kernel(x_prompt, x_sample, cache_k_win, cache_v_win, state_conv_b, state_conv_ffn, norm1_g, w_in, sinks,
           w_o_attn, conv_b_w, conv_b_bias, ln_b_g, ln_b_b, w_pb, w_out, norm2_g, w_up, ffn_conv_w,
           ffn_conv_b, w_down, norm_f_g):
    assert w_in.shape[0] == 1, "single-layer stack only"
    res = _layer(x_prompt, x_sample, cache_k_win[0], cache_v_win[0], state_conv_b[0], state_conv_ffn[0],
                 norm1_g[0], w_in[0], sinks[0], w_o_attn[0], conv_b_w[0], conv_b_bias[0], ln_b_g[0],
                 ln_b_b[0], w_pb[0], w_out[0], norm2_g[0], w_up[0], ffn_conv_w[0], ffn_conv_b[0],
                 w_down[0], norm_f_g, tm_a=256, tm_ffn=512, gs=16)
    return res[:2] + tuple(r[None] for r in res[2:])
```

```python
import functools

import jax
import jax.numpy as jnp
from jax import lax
from jax.experimental import pallas as pl
from jax.experimental.pallas import tpu as pltpu

F32 = jnp.float32
BF16 = jnp.bfloat16

D = 1024
N_HEADS, N_KV, HEAD_DIM = 16, 4, 64
WINDOW = 128
KV_DIM = N_KV * HEAD_DIM
CONV_W = 31
D_FF = 2816
FFN_CONV_W = 3
EPS = 1e-6
NEG_INF = -1e30
SCALE = HEAD_DIM ** -0.5
QKV_W = D + 2 * KV_DIM
GLU_OFF = QKV_W
GATE_OFF = QKV_W + 2 * D
IN_DIM = QKV_W + 4 * D
LANES = 128
SUBLANES = 8
GROUP = N_HEADS // N_KV
ATTN_BATCH = 4
FFN_CH = 256
FFN_NC = D_FF // FFN_CH
VMEM_LIMIT = 56 * 1024 * 1024


def _resident(shape):
    nd = len(shape)
    return pl.BlockSpec(shape, lambda *_: (0,) * nd, pipeline_mode=pl.Buffered(1))


def _sigmoid(x):
    return 1.0 / (1.0 + jnp.exp(-x))


def _rmsnorm(x, g):
    return x * lax.rsqrt(jnp.mean(x * x, axis=-1, keepdims=True) + EPS) * g


def _layernorm(x, g, b):
    mu = jnp.mean(x, axis=-1, keepdims=True)
    xc = x - mu
    var = jnp.mean(xc * xc, axis=-1, keepdims=True)
    return xc * lax.rsqrt(var + EPS) * g + b


def _dot(a, b):
    return jnp.dot(a, b, preferred_element_type=F32)


def _softmax_with_sink(s, mask, sink):
    s = jnp.where(mask, s, NEG_INF)
    m = jnp.maximum(jnp.max(s, axis=-1, keepdims=True), sink)
    e = jnp.exp(s - m)
    den = jnp.sum(e, axis=-1, keepdims=True) + jnp.exp(sink - m)
    return e / den


def _split_heads_block_diag(x, lo):
    xr = pltpu.roll(x, HEAD_DIM, axis=1)
    z = jnp.zeros_like(x)
    even = (jnp.where(lo, x, z), jnp.where(lo, z, xr))
    odd = (jnp.where(lo, xr, z), jnp.where(lo, z, x))
    return even, odd


def _prompt_a_body(x_ref, g1_ref, w_in_ref, sinks_ref, w_o_ref, cw_ref, cb_ref, lng_ref, lnb_ref,
                   w_pb_ref, w_out_ref,
                   h_ref, kwin_ref, vwin_ref, cbs_ref,
                   kbd_scr, vbd_scr, q_scr, attn_scr, xn_scr, ext_scr, carry_scr, c_scr, *, tm):
    t = pl.program_id(1)
    last = pl.num_programs(1) - 1

    @pl.when(t == 0)
    def _zero_history():
        kbd_scr[:, :, 0:WINDOW, :] = jnp.zeros((N_KV, 2, WINDOW, LANES), BF16)
        vbd_scr[:, :, 0:WINDOW, :] = jnp.zeros((N_KV, 2, WINDOW, LANES), BF16)
        carry_scr[...] = jnp.zeros(carry_scr.shape, F32)

    x = x_ref[0]
    xn_f32 = _rmsnorm(x, g1_ref[...])
    xn = xn_f32.astype(BF16)

    qkv = _dot(xn, w_in_ref[:, 0:QKV_W])
    q_scr[...] = qkv[:, 0:D].astype(BF16)
    k_new = qkv[:, D:D + KV_DIM]
    v_new = qkv[:, D + KV_DIM:QKV_W]

    @pl.when(t == last)
    def _emit_kv_window():
        kwin_ref[0] = k_new[tm - WINDOW:, :]
        vwin_ref[0] = v_new[tm - WINDOW:, :]

    lo = lax.broadcasted_iota(jnp.int32, (tm, LANES), 1) < HEAD_DIM
    for m in range(N_KV // 2):
        k_even, k_odd = _split_heads_block_diag(k_new[:, LANES * m:LANES * (m + 1)] * SCALE, lo)
        v_even, v_odd = _split_heads_block_diag(v_new[:, LANES * m:LANES * (m + 1)], lo)
        for half in range(2):
            kbd_scr[2 * m, half, WINDOW:, :] = k_even[half].astype(BF16)
            kbd_scr[2 * m + 1, half, WINDOW:, :] = k_odd[half].astype(BF16)
            vbd_scr[2 * m, half, WINDOW:, :] = v_even[half].astype(BF16)
            vbd_scr[2 * m + 1, half, WINDOW:, :] = v_odd[half].astype(BF16)

    row = lax.broadcasted_iota(jnp.int32, (WINDOW, 2 * WINDOW), 0)
    col = lax.broadcasted_iota(jnp.int32, (WINDOW, 2 * WINDOW), 1)
    band = jnp.logical_and(col > row, col <= row + WINDOW)
    first_mask = jnp.logical_and(band, col >= jnp.where(t > 0, 0, WINDOW))

    def scores(qb, g):
        r0 = qb * WINDOW
        c0 = 2 * LANES * g
        q_pairs = jnp.concatenate([q_scr[r0:r0 + WINDOW, c0:c0 + LANES],
                                   q_scr[r0:r0 + WINDOW, c0 + LANES:c0 + 2 * LANES]], axis=0)
        kbd = jnp.concatenate([kbd_scr[g, 0, r0:r0 + 2 * WINDOW, :],
                               kbd_scr[g, 1, r0:r0 + 2 * WINDOW, :]], axis=0)
        return lax.dot_general(q_pairs, kbd, (((1,), (1,)), ((), ())), preferred_element_type=F32)

    def attend(item, s):
        qb, g = item
        r0 = qb * WINDOW
        c0 = 2 * LANES * g
        mask = first_mask if qb == 0 else band
        probs = []
        for pair in range(2):
            halves = []
            for hh in range(2):
                sink = sinks_ref[GROUP * g + 2 * pair + hh]
                p = _softmax_with_sink(s[pair * WINDOW:(pair + 1) * WINDOW, 2 * WINDOW * hh:2 * WINDOW * (hh + 1)],
                                       mask, sink)
                halves.append(p.astype(BF16))
            probs.append(jnp.concatenate(halves, axis=1))
        vbd = jnp.concatenate([vbd_scr[g, 0, r0:r0 + 2 * WINDOW, :],
                               vbd_scr[g, 1, r0:r0 + 2 * WINDOW, :]], axis=0)
        o = _dot(jnp.concatenate(probs, axis=0), vbd)
        attn_scr[r0:r0 + WINDOW, c0:c0 + LANES] = o[0:WINDOW, :].astype(BF16)
        attn_scr[r0:r0 + WINDOW, c0 + LANES:c0 + 2 * LANES] = o[WINDOW:2 * WINDOW, :].astype(BF16)

    items = [(qb, g) for qb in range(tm // WINDOW) for g in range(N_KV)]
    batches = [items[i:i + ATTN_BATCH] for i in range(0, len(items), ATTN_BATCH)]
    s_next = [scores(*it) for it in batches[0]]
    for k, batch in enumerate(batches):
        s_cur = s_next
        if k + 1 < len(batches):
            s_next = [scores(*it) for it in batches[k + 1]]
        for it, s in zip(batch, s_cur):
            attend(it, s)

    kbd_scr[:, :, 0:WINDOW, :] = kbd_scr[:, :, tm:tm + WINDOW, :]
    vbd_scr[:, :, 0:WINDOW, :] = vbd_scr[:, :, tm:tm + WINDOW, :]
    branch_a = _dot(attn_scr[...], w_o_ref[...])

    n_tiles = tm // SUBLANES
    assert n_tiles >= CONV_W - 1, "a tap must not reach further back than the previous-sublane tiles"
    xn_scr[...] = xn_f32.reshape(SUBLANES, n_tiles, D)
    xn_perm = jnp.concatenate([xn_scr[:, p, :] for p in range(n_tiles)], axis=0).astype(BF16)
    glu = _dot(xn_perm, w_in_ref[:, GLU_OFF:GATE_OFF])
    u_perm = (glu[:, 0:D] * _sigmoid(glu[:, D:2 * D])).reshape(n_tiles, SUBLANES, D)
    ext_scr[n_tiles:2 * n_tiles] = u_perm
    prev_last = jnp.broadcast_to(carry_scr[...][:, None, :], (n_tiles, SUBLANES, D))
    first_sub = lax.broadcasted_iota(jnp.int32, (n_tiles, SUBLANES, D), 1) == 0
    ext_scr[0:n_tiles] = jnp.where(first_sub, prev_last, pltpu.roll(u_perm, 1, axis=1))

    @pl.when(t == last)
    def _emit_conv_state():
        cbs_ref[0] = ext_scr[2 * n_tiles - (CONV_W - 1):2 * n_tiles, SUBLANES - 1, :]

    carry_scr[...] = ext_scr[n_tiles:2 * n_tiles, SUBLANES - 1, :]

    tiles = 8
    lane_chunk = 128

    def _conv_rows(i, carry):
        p0 = i * tiles
        for lc in range(D // lane_chunk):
            sl = slice(lc * lane_chunk, (lc + 1) * lane_chunk)
            acc = jnp.broadcast_to(cb_ref[:, sl], (tiles * SUBLANES, lane_chunk))
            for j in range(CONV_W):
                back = CONV_W - 1 - j
                w = jnp.concatenate([cw_ref[j, :, sl]] * tiles, axis=0)
                win = ext_scr[pl.ds(n_tiles + p0 - back, tiles), :, sl]
                acc = acc + win.reshape(tiles * SUBLANES, lane_chunk) * w
            c_scr[pl.ds(p0, tiles), :, sl] = acc.reshape(tiles, SUBLANES, lane_chunk)
        return carry

    lax.fori_loop(0, n_tiles // tiles, _conv_rows, 0)

    y = _layernorm(c_scr[...].reshape(tm, D), lng_ref[...], lnb_ref[...])
    c_scr[...] = _dot((y * _sigmoid(y)).astype(BF16), w_pb_ref[...]).reshape(n_tiles, SUBLANES, D)
    branch_b = jnp.concatenate(
        [c_scr[(n * SUBLANES) % n_tiles:(n * SUBLANES) % n_tiles + SUBLANES, (n * SUBLANES) // n_tiles, :]
         for n in range(n_tiles)], axis=0)

    gates = _dot(xn, w_in_ref[:, GATE_OFF:IN_DIM])
    merged = _sigmoid(gates[:, 0:D]) * branch_a + _sigmoid(gates[:, D:2 * D]) * branch_b
    h_ref[0] = x + _dot(merged.astype(BF16), w_out_ref[...])


def _prompt_stage_a(x, g1, w_in, sinks, w_o, cw, cb, lng, lnb, w_pb, w_out, *, tm):
    b, s, _ = x.shape
    grid = (b, s // tm)
    row_block = pl.BlockSpec((1, tm, D), lambda i, t: (i, t, 0))
    per_seq = lambda r, c: pl.BlockSpec((1, r, c), lambda i, t: (i, 0, 0))
    return pl.pallas_call(
        functools.partial(_prompt_a_body, tm=tm),
        grid=grid,
        in_specs=[row_block, _resident((1, D)), _resident((D, IN_DIM)),
                  pl.BlockSpec(memory_space=pltpu.SMEM),
                  _resident((D, D)), _resident((CONV_W, SUBLANES, D)), _resident((1, D)), _resident((1, D)),
                  _resident((1, D)), _resident((D, D)), _resident((D, D))],
        out_specs=[row_block, per_seq(WINDOW, KV_DIM), per_seq(WINDOW, KV_DIM), per_seq(CONV_W - 1, D)],
        out_shape=[jax.ShapeDtypeStruct((b, s, D), F32),
                   jax.ShapeDtypeStruct((b, WINDOW, KV_DIM), F32),
                   jax.ShapeDtypeStruct((b, WINDOW, KV_DIM), F32),
                   jax.ShapeDtypeStruct((b, CONV_W - 1, D), F32)],
        scratch_shapes=[pltpu.VMEM((N_KV, 2, WINDOW + tm, LANES), BF16),
                        pltpu.VMEM((N_KV, 2, WINDOW + tm, LANES), BF16),
                        pltpu.VMEM((tm, D), BF16),
                        pltpu.VMEM((tm, D), BF16),
                        pltpu.VMEM((SUBLANES, tm // SUBLANES, D), F32),
                        pltpu.VMEM((2 * tm // SUBLANES, SUBLANES, D), F32),
                        pltpu.VMEM((tm // SUBLANES, D), F32),
                        pltpu.VMEM((tm // SUBLANES, SUBLANES, D), F32)],
        compiler_params=pltpu.CompilerParams(dimension_semantics=("arbitrary", "arbitrary"),
                                             vmem_limit_bytes=VMEM_LIMIT),
        name="prompt_stage_a",
    )(x, g1, w_in, sinks, w_o, cw, cb, lng, lnb, w_pb, w_out)


def _ffn_chunks(h, hn, prev_rows, w_up_ref, fcw_ref, fcb_ref, wd_ref):
    def up_chunk(c):
        halves = []
        for part in range(2):
            cols = slice(part * D_FF + c * FFN_CH, part * D_FF + (c + 1) * FFN_CH)
            up = _dot(hn, w_up_ref[:, cols])
            prev1, prev2 = prev_rows(cols, up)
            w = fcw_ref[:, cols]
            halves.append(w[0:1, :] * prev2 + w[1:2, :] * prev1 + w[2:3, :] * up + fcb_ref[:, cols])
        return halves

    acc = h
    nxt = up_chunk(0)
    for c in range(FFN_NC):
        gate, val = nxt
        if c + 1 < FFN_NC:
            nxt = up_chunk(c + 1)
        act = (gate * _sigmoid(gate) * val).astype(BF16)
        acc = acc + _dot(act, wd_ref[c * FFN_CH:(c + 1) * FFN_CH, :])
    return acc


def _prompt_ffn_body(h_ref, g2_ref, w_up_ref, fcw_ref, fcb_ref, wd_ref, gf_ref,
                     y_ref, state_ref, carry_scr, *, tm):
    t = pl.program_id(1)

    @pl.when(t == 0)
    def _zero_history():
        carry_scr[...] = jnp.zeros(carry_scr.shape, F32)

    h = h_ref[0]
    hn = _rmsnorm(h, g2_ref[...]).astype(BF16)

    def prev_rows(cols, up):
        ext = jnp.concatenate([carry_scr[:, cols], up], axis=0)
        carry_scr[:, cols] = up[tm - SUBLANES:, :]
        state_ref[0, :, cols] = up[tm - (FFN_CONV_W - 1):, :]
        return ext[SUBLANES - 1:SUBLANES - 1 + tm, :], ext[SUBLANES - 2:SUBLANES - 2 + tm, :]

    y = _ffn_chunks(h, hn, prev_rows, w_up_ref, fcw_ref, fcb_ref, wd_ref)
    y_ref[0] = _rmsnorm(y, gf_ref[...])


def _ffn_weight_specs():
    return [_resident((1, D)), _resident((D, 2 * D_FF)), _resident((FFN_CONV_W, 2 * D_FF)),
            _resident((1, 2 * D_FF)), _resident((D_FF, D)), _resident((1, D))]


def _prompt_ffn(h, ffn_w, *, tm):
    b, s, _ = h.shape
    row_block = pl.BlockSpec((1, tm, D), lambda i, t: (i, t, 0))
    state_shape = (b, FFN_CONV_W - 1, 2 * D_FF)
    return pl.pallas_call(
        functools.partial(_prompt_ffn_body, tm=tm),
        grid=(b, s // tm),
        in_specs=[row_block] + _ffn_weight_specs(),
        out_specs=[row_block, pl.BlockSpec((1,) + state_shape[1:], lambda i, t: (i, 0, 0))],
        out_shape=[jax.ShapeDtypeStruct((b, s, D), F32), jax.ShapeDtypeStruct(state_shape, F32)],
        scratch_shapes=[pltpu.VMEM((SUBLANES, 2 * D_FF), F32)],
        compiler_params=pltpu.CompilerParams(dimension_semantics=("arbitrary", "arbitrary"),
                                             vmem_limit_bytes=VMEM_LIMIT),
        name="prompt_ffn",
    )(h, *ffn_w)


def _sample_proj_body(x_ref, g1_ref, w_in_ref, q_ref, kvt_ref, u_ref, gates_ref):
    xn = _rmsnorm(x_ref[...], g1_ref[...]).astype(BF16)
    q_ref[...] = _dot(xn, w_in_ref[:, 0:D]) * SCALE
    kvt_ref[...] = lax.dot_general(w_in_ref[:, D:QKV_W], xn, (((0,), (1,)), ((), ())),
                                   preferred_element_type=F32)
    glu = _dot(xn, w_in_ref[:, GLU_OFF:GATE_OFF])
    u_ref[...] = glu[:, 0:D] * _sigmoid(glu[:, D:2 * D])
    gates_ref[...] = _sigmoid(_dot(xn, w_in_ref[:, GATE_OFF:IN_DIM]))


def _sample_proj(x2d, g1, w_in):
    m = x2d.shape[0]
    full = lambda r, c: pl.BlockSpec((r, c), lambda i: (0, 0))
    return pl.pallas_call(
        _sample_proj_body,
        grid=(1,),
        in_specs=[full(m, D), _resident((1, D)), _resident((D, IN_DIM))],
        out_specs=[full(m, D), full(2 * KV_DIM, m), full(m, D), full(m, 2 * D)],
        out_shape=[jax.ShapeDtypeStruct((m, D), F32), jax.ShapeDtypeStruct((2 * KV_DIM, m), F32),
                   jax.ShapeDtypeStruct((m, D), F32), jax.ShapeDtypeStruct((m, 2 * D), F32)],
        compiler_params=pltpu.CompilerParams(dimension_semantics=("arbitrary",),
                                             vmem_limit_bytes=VMEM_LIMIT),
        name="sample_proj",
    )(x2d, g1, w_in)


def _sample_mix_body(q_ref, kvt_ref, u_ref, kt_ref, vt_ref, cstate_ref, sink_ref,
                     cw_ref, cb_ref, lng_ref, lnb_ref,
                     attn_ref, cs_ref, kt_out_ref, vt_out_ref, cstate_out_ref,
                     qg_scr, *, gs, n_tok):
    rows_per_group = GROUP * n_tok
    keep = WINDOW - n_tok

    lane = lax.broadcasted_iota(jnp.int32, (N_KV, HEAD_DIM, WINDOW), 2)
    for n in range(gs):
        shift = (keep - n_tok * n) % WINDOW
        new_k = pltpu.roll(kvt_ref[0:KV_DIM, :], shift, axis=1).reshape(N_KV, HEAD_DIM, WINDOW)
        new_v = pltpu.roll(kvt_ref[KV_DIM:2 * KV_DIM, :], shift, axis=1).reshape(N_KV, HEAD_DIM, WINDOW)
        kt_out_ref[n] = jnp.where(lane >= keep, new_k, pltpu.roll(kt_ref[n], keep, axis=2))
        vt_out_ref[n] = jnp.where(lane >= keep, new_v, pltpu.roll(vt_ref[n], keep, axis=2))

    tok = lax.broadcasted_iota(jnp.int32, (rows_per_group, WINDOW), 0) % n_tok
    key = lax.broadcasted_iota(jnp.int32, (rows_per_group, WINDOW), 1)
    mask_old = (key > tok)[None]
    mask_new = jnp.logical_and(key >= keep, key - keep <= tok)[None]
    for g in range(N_KV):
        for hh in range(GROUP):
            h0 = (GROUP * g + hh) * HEAD_DIM
            qg_scr[:, hh * n_tok:(hh + 1) * n_tok, :] = q_ref[:, h0:h0 + HEAD_DIM].reshape(gs, n_tok, HEAD_DIM)
        qg = qg_scr[...].astype(BF16)
        s_old = jnp.einsum("nqd,ndk->nqk", qg, kt_ref[:, g].astype(BF16), preferred_element_type=F32)
        s_new = jnp.einsum("nqd,ndk->nqk", qg, kt_out_ref[:, g].astype(BF16), preferred_element_type=F32)
        s_old = jnp.where(mask_old, s_old, NEG_INF)
        s_new = jnp.where(mask_new, s_new, NEG_INF)
        sink = sink_ref[g][None]
        m = jnp.maximum(jnp.maximum(jnp.max(s_old, axis=-1, keepdims=True),
                                    jnp.max(s_new, axis=-1, keepdims=True)), sink)
        e_old = jnp.exp(s_old - m)
        e_new = jnp.exp(s_new - m)
        den = (jnp.sum(e_old, axis=-1, keepdims=True) + jnp.sum(e_new, axis=-1, keepdims=True)
               + jnp.exp(sink - m))
        o = (jnp.einsum("nqk,ndk->nqd", (e_old / den).astype(BF16), vt_ref[:, g].astype(BF16),
                        preferred_element_type=F32)
             + jnp.einsum("nqk,ndk->nqd", (e_new / den).astype(BF16), vt_out_ref[:, g].astype(BF16),
                          preferred_element_type=F32))
        for hh in range(GROUP):
            h0 = (GROUP * g + hh) * HEAD_DIM
            attn_ref[:, h0:h0 + HEAD_DIM] = o[:, hh * n_tok:(hh + 1) * n_tok, :].reshape(gs * n_tok, HEAD_DIM)

    n_state = CONV_W - 1
    u_slabs = [u_ref[:, i, :] for i in range(n_tok)]

    def ext(r, sl):
        return cstate_ref[r, :, sl] if r < n_state else u_slabs[r - n_state][:, sl]

    for r in range(n_state):
        cstate_out_ref[r] = ext(r + n_tok, slice(None))

    lane_chunk = 256
    for i in range(n_tok):
        pieces = []
        for lc in range(D // lane_chunk):
            sl = slice(lc * lane_chunk, (lc + 1) * lane_chunk)
            acc = jnp.broadcast_to(cb_ref[:, sl], (gs, lane_chunk))
            for j in range(CONV_W):
                w = jnp.concatenate([cw_ref[j, :, sl]] * (gs // SUBLANES), axis=0)
                acc = acc + ext(i + j, sl) * w
            pieces.append(acc)
        y = _layernorm(jnp.concatenate(pieces, axis=1), lng_ref[...], lnb_ref[...])
        cs_ref[:, i, :] = y * _sigmoid(y)


def _sample_mix(q, kvt, u, kt, vt, cstate, sink_rows, cw, cb, lng, lnb, *, gs):
    n_seq, n_tok, _ = u.shape
    rows = gs * n_tok
    assert rows % LANES == 0, "a group's token columns must fill whole lane tiles of the transposed k/v"
    row_block = pl.BlockSpec((rows, D), lambda s: (s, 0))
    seq_block = pl.BlockSpec((gs, n_tok, D), lambda s: (s, 0, 0))
    cache_block = pl.BlockSpec((gs, N_KV, HEAD_DIM, WINDOW), lambda s: (s, 0, 0, 0))
    state_block = pl.BlockSpec((CONV_W - 1, gs, D), lambda s: (0, s, 0))
    rows_per_group = GROUP * n_tok
    return pl.pallas_call(
        functools.partial(_sample_mix_body, gs=gs, n_tok=n_tok),
        grid=(n_seq // gs,),
        in_specs=[row_block, pl.BlockSpec((2 * KV_DIM, rows), lambda s: (0, s)), seq_block,
                  cache_block, cache_block, state_block,
                  _resident((N_KV, rows_per_group, 1)),
                  _resident((CONV_W, SUBLANES, D)), _resident((1, D)), _resident((1, D)), _resident((1, D))],
        out_specs=[row_block, seq_block, cache_block, cache_block, state_block],
        out_shape=[jax.ShapeDtypeStruct((n_seq * n_tok, D), F32), jax.ShapeDtypeStruct((n_seq, n_tok, D), F32),
                   jax.ShapeDtypeStruct((n_seq, N_KV, HEAD_DIM, WINDOW), F32),
                   jax.ShapeDtypeStruct((n_seq, N_KV, HEAD_DIM, WINDOW), F32),
                   jax.ShapeDtypeStruct((CONV_W - 1, n_seq, D), F32)],
        scratch_shapes=[pltpu.VMEM((gs, rows_per_group, HEAD_DIM), F32)],
        compiler_params=pltpu.CompilerParams(dimension_semantics=("arbitrary",),
                                             vmem_limit_bytes=VMEM_LIMIT),
        name="sample_mix",
    )(q, kvt, u, kt, vt, cstate, sink_rows, cw, cb, lng, lnb)


def _sample_merge_body(x_ref, attn_ref, cs_ref, gates_ref, w_o_ref, w_pb_ref, w_out_ref, h_ref):
    x = x_ref[...]
    branch_a = _dot(attn_ref[...].astype(BF16), w_o_ref[...])
    branch_b = _dot(cs_ref[...].astype(BF16), w_pb_ref[...])
    merged = gates_ref[:, 0:D] * branch_a + gates_ref[:, D:2 * D] * branch_b
    h_ref[...] = x + _dot(merged.astype(BF16), w_out_ref[...])


def _sample_merge(x, attn, cs, gates, w_o, w_pb, w_out):
    m = x.shape[0]
    full = lambda c: pl.BlockSpec((m, c), lambda i: (0, 0))
    return pl.pallas_call(
        _sample_merge_body,
        grid=(1,),
        in_specs=[full(D), full(D), full(D), full(2 * D),
                  _resident((D, D)), _resident((D, D)), _resident((D, D))],
        out_specs=full(D),
        out_shape=jax.ShapeDtypeStruct((m, D), F32),
        compiler_params=pltpu.CompilerParams(dimension_semantics=("arbitrary",),
                                             vmem_limit_bytes=VMEM_LIMIT),
        name="sample_merge",
    )(x, attn, cs, gates, w_o, w_pb, w_out)


def _sample_ffn_body(h_ref, state_ref, g2_ref, w_up_ref, fcw_ref, fcb_ref, wd_ref, gf_ref,
                     y_ref, state_out_ref, *, n_seq, n_tok):
    m = n_seq * n_tok
    h = jnp.concatenate([h_ref[:, i, :] for i in range(n_tok)], axis=0)
    hn = _rmsnorm(h, g2_ref[...]).astype(BF16)

    def prev_rows(cols, up):
        s0 = state_ref[0, :, cols]
        s1 = state_ref[1, :, cols]
        state_out_ref[0, :, cols] = up[m - 2 * n_seq:m - n_seq, :]
        state_out_ref[1, :, cols] = up[m - n_seq:, :]
        prev1 = jnp.concatenate([s1, up[:m - n_seq, :]], axis=0)
        prev2 = jnp.concatenate([s0, s1, up[:m - 2 * n_seq, :]], axis=0)
        return prev1, prev2

    y = _ffn_chunks(h, hn, prev_rows, w_up_ref, fcw_ref, fcb_ref, wd_ref)
    y = _rmsnorm(y, gf_ref[...])
    for i in range(n_tok):
        y_ref[:, i, :] = y[i * n_seq:(i + 1) * n_seq, :]


def _sample_ffn(h, state, ffn_w, *, n_seq, n_tok):
    m = n_seq * n_tok
    state_shape = (FFN_CONV_W - 1, n_seq, 2 * D_FF)
    state_spec = pl.BlockSpec(state_shape, lambda i: (0, 0, 0))
    seq_major = pl.BlockSpec((n_seq, n_tok, D), lambda i: (0, 0, 0))
    return pl.pallas_call(
        functools.partial(_sample_ffn_body, n_seq=n_seq, n_tok=n_tok),
        grid=(1,),
        in_specs=[seq_major, state_spec] + _ffn_weight_specs(),
        out_specs=[seq_major, state_spec],
        out_shape=[jax.ShapeDtypeStruct((n_seq, n_tok, D), F32), jax.ShapeDtypeStruct(state_shape, F32)],
        compiler_params=pltpu.CompilerParams(dimension_semantics=("arbitrary",),
                                             vmem_limit_bytes=VMEM_LIMIT),
        name="sample_ffn",
    )(h, state, *ffn_w)


def _layer(x_prompt, x_sample, k_past, v_past, conv_b_past, conv_f_past,
           norm1_g, w_in, sinks, w_o_attn, conv_b_w, conv_b_bias, ln_b_g, ln_b_b, w_pb, w_out,
           norm2_g, w_up, ffn_conv_w, ffn_conv_b, w_down, norm_f_g, *, tm_a, tm_ffn, gs):
    n_seq, n_tok, _ = x_sample.shape
    row = lambda v: v.reshape(1, -1)
    g1, g2, gf = row(norm1_g), row(norm2_g), row(norm_f_g)
    cb, lng, lnb = row(conv_b_bias), row(ln_b_g), row(ln_b_b)
    w_in_b, w_o_b, w_pb_b, w_out_b = (w.astype(BF16) for w in (w_in, w_o_attn, w_pb, w_out))
    ffn_w = (g2, w_up.astype(BF16), ffn_conv_w, row(ffn_conv_b), w_down.astype(BF16), gf)

    cw8 = jnp.broadcast_to(conv_b_w[:, None, :], (CONV_W, SUBLANES, D))
    h_p, k_win_p, v_win_p, conv_b_p = _prompt_stage_a(
        x_prompt, g1, w_in_b, sinks, w_o_b, cw8, cb, lng, lnb, w_pb_b, w_out_b, tm=tm_a)
    y_p, conv_f_p = _prompt_ffn(h_p, ffn_w, tm=tm_ffn)

    m = n_seq * n_tok
    q, kvt, u, gates = _sample_proj(x_sample.reshape(m, D), g1, w_in_b)
    sink_rows = jnp.repeat(sinks.reshape(N_KV, GROUP), n_tok, axis=1)[:, :, None]
    attn, cs, kt_s, vt_s, cstate_s = _sample_mix(
        q, kvt, u.reshape(n_seq, n_tok, D),
        k_past.transpose(0, 2, 3, 1), v_past.transpose(0, 2, 3, 1), conv_b_past.transpose(1, 0, 2),
        sink_rows, cw8, cb, lng, lnb, gs=gs)
    conv_b_s = cstate_s.transpose(1, 0, 2)
    h_s = _sample_merge(x_sample.reshape(m, D), attn, cs.reshape(m, D), gates, w_o_b, w_pb_b, w_out_b)
    y_s, ffn_state_s = _sample_ffn(h_s.reshape(n_seq, n_tok, D), conv_f_past.transpose(1, 0, 2), ffn_w,
                                   n_seq=n_seq, n_tok=n_tok)
    conv_f_s = ffn_state_s.transpose(1, 0, 2)

    kv_shape = lambda a: a.reshape(a.shape[0], WINDOW, N_KV, HEAD_DIM)
    return (y_p, y_s,
            kv_shape(k_win_p), kv_shape(v_win_p), conv_b_p, conv_f_p,
            kt_s.transpose(0, 3, 1, 2), vt_s.transpose(0, 3, 1, 2), conv_b_s, conv_f_s)


def kernel(x_prompt, x_sample, cache_k_win, cache_v_win, state_conv_b, state_conv_ffn, norm1_g, w_in, sinks,
           w_o_attn, conv_b_w, conv_b_bias, ln_b_g, ln_b_b, w_pb, w_out, norm2_g, w_up, ffn_conv_w,
           ffn_conv_b, w_down, norm_f_g):
    assert w_in.shape[0] == 1, "single-layer stack only"
    res = _layer(x_prompt, x_sample, cache_k_win[0], cache_v_win[0], state_conv_b[0], state_conv_ffn[0],
                 norm1_g[0], w_in[0], sinks[0], w_o_attn[0], conv_b_w[0], conv_b_bias[0], ln_b_g[0],
                 ln_b_b[0], w_pb[0], w_out[0], norm2_g[0], w_up[0], ffn_conv_w[0], ffn_conv_b[0],
                 w_down[0], norm_f_g, tm_a=256, tm_ffn=512, gs=16)
    return res[:2] + tuple(r[None] for r in res[2:])
```

```python
import functools

import jax
import jax.numpy as jnp
from jax import lax
from jax.experimental import pallas as pl
from jax.experimental.pallas import tpu as pltpu

F32 = jnp.float32
BF16 = jnp.bfloat16

D = 1024
N_HEADS, N_KV, HEAD_DIM = 16, 4, 64
WINDOW = 128
KV_DIM = N_KV * HEAD_DIM
CONV_W = 31
D_FF = 2816
FFN_CONV_W = 3
EPS = 1e-6
NEG_INF = -1e30
SCALE = HEAD_DIM ** -0.5
QKV_W = D + 2 * KV_DIM
GLU_OFF = QKV_W
GATE_OFF = QKV_W + 2 * D
IN_DIM = QKV_W + 4 * D
LANES = 128
SUBLANES = 8
GROUP = N_HEADS // N_KV
ATTN_BATCH = 4
CONV_STEPS = 4
FFN_CH = 256
FFN_NC = D_FF // FFN_CH
FFN_LOOKAHEAD = 3
VMEM_LIMIT = 56 * 1024 * 1024


def _resident(shape):
    nd = len(shape)
    return pl.BlockSpec(shape, lambda *_: (0,) * nd, pipeline_mode=pl.Buffered(1))


def _sigmoid(x):
    return 1.0 / (1.0 + jnp.exp(-x))


def _rmsnorm(x, g):
    return x * lax.rsqrt(jnp.mean(x * x, axis=-1, keepdims=True) + EPS) * g


def _layernorm(x, g, b):
    mu = jnp.mean(x, axis=-1, keepdims=True)
    xc = x - mu
    var = jnp.mean(xc * xc, axis=-1, keepdims=True)
    return xc * lax.rsqrt(var + EPS) * g + b


def _dot(a, b):
    return jnp.dot(a, b, preferred_element_type=F32)


def _softmax_with_sink(s, mask, sink):
    s = jnp.where(mask, s, NEG_INF)
    m = jnp.maximum(jnp.max(s, axis=-1, keepdims=True), sink)
    e = jnp.exp(s - m)
    den = jnp.sum(e, axis=-1, keepdims=True) + jnp.exp(sink - m)
    return e / den


def _split_heads_block_diag(x, lo):
    xr = pltpu.roll(x, HEAD_DIM, axis=1)
    z = jnp.zeros_like(x)
    even = (jnp.where(lo, x, z), jnp.where(lo, z, xr))
    odd = (jnp.where(lo, xr, z), jnp.where(lo, z, x))
    return even, odd


def _prompt_a_body(x_ref, g1_ref, w_in_ref, w_gate_ref, sinks_ref, w_o_ref, cw_ref, cb_ref, lng_ref, lnb_ref,
                   w_pb_ref, w_out_ref,
                   h_ref, kwin_ref, vwin_ref, cbs_ref,
                   kbd_scr, vbd_scr, q_scr, attn_scr, xn_scr, ext_scr, carry_scr, c_scr, xnb_scr, gates_scr,
                   *, tm):
    t = pl.program_id(1)
    last = pl.num_programs(1) - 1

    @pl.when(t == 0)
    def _zero_history():
        kbd_scr[:, :, 0:WINDOW, :] = jnp.zeros((N_KV, 2, WINDOW, LANES), BF16)
        vbd_scr[:, :, 0:WINDOW, :] = jnp.zeros((N_KV, 2, WINDOW, LANES), BF16)
        carry_scr[...] = jnp.zeros(carry_scr.shape, F32)

    x = x_ref[0]
    xn_f32 = _rmsnorm(x, g1_ref[...])
    xn = xn_f32.astype(BF16)
    xnb_scr[...] = xn

    qkv = _dot(xn, w_in_ref[:, 0:QKV_W])
    q_scr[...] = qkv[:, 0:D].astype(BF16)
    k_new = qkv[:, D:D + KV_DIM]
    v_new = qkv[:, D + KV_DIM:QKV_W]

    @pl.when(t == last)
    def _emit_kv_window():
        kwin_ref[0] = k_new[tm - WINDOW:, :]
        vwin_ref[0] = v_new[tm - WINDOW:, :]

    lo = lax.broadcasted_iota(jnp.int32, (tm, LANES), 1) < HEAD_DIM
    for m in range(N_KV // 2):
        k_even, k_odd = _split_heads_block_diag(k_new[:, LANES * m:LANES * (m + 1)] * SCALE, lo)
        v_even, v_odd = _split_heads_block_diag(v_new[:, LANES * m:LANES * (m + 1)], lo)
        for half in range(2):
            kbd_scr[2 * m, half, WINDOW:, :] = k_even[half].astype(BF16)
            kbd_scr[2 * m + 1, half, WINDOW:, :] = k_odd[half].astype(BF16)
            vbd_scr[2 * m, half, WINDOW:, :] = v_even[half].astype(BF16)
            vbd_scr[2 * m + 1, half, WINDOW:, :] = v_odd[half].astype(BF16)

    row = lax.broadcasted_iota(jnp.int32, (WINDOW, 2 * WINDOW), 0)
    col = lax.broadcasted_iota(jnp.int32, (WINDOW, 2 * WINDOW), 1)
    band = jnp.logical_and(col > row, col <= row + WINDOW)
    first_mask = jnp.logical_and(band, col >= jnp.where(t > 0, 0, WINDOW))

    def scores(qb, g):
        r0 = qb * WINDOW
        c0 = 2 * LANES * g
        q_pairs = jnp.concatenate([q_scr[r0:r0 + WINDOW, c0:c0 + LANES],
                                   q_scr[r0:r0 + WINDOW, c0 + LANES:c0 + 2 * LANES]], axis=0)
        kbd = jnp.concatenate([kbd_scr[g, 0, r0:r0 + 2 * WINDOW, :],
                               kbd_scr[g, 1, r0:r0 + 2 * WINDOW, :]], axis=0)
        return lax.dot_general(q_pairs, kbd, (((1,), (1,)), ((), ())), preferred_element_type=F32)

    def attend(item, s):
        qb, g = item
        r0 = qb * WINDOW
        c0 = 2 * LANES * g
        mask = first_mask if qb == 0 else band
        probs = []
        for pair in range(2):
            halves = []
            for hh in range(2):
                sink = sinks_ref[GROUP * g + 2 * pair + hh]
                p = _softmax_with_sink(s[pair * WINDOW:(pair + 1) * WINDOW, 2 * WINDOW * hh:2 * WINDOW * (hh + 1)],
                                       mask, sink)
                halves.append(p.astype(BF16))
            probs.append(jnp.concatenate(halves, axis=1))
        vbd = jnp.concatenate([vbd_scr[g, 0, r0:r0 + 2 * WINDOW, :],
                               vbd_scr[g, 1, r0:r0 + 2 * WINDOW, :]], axis=0)
        o = _dot(jnp.concatenate(probs, axis=0), vbd)
        attn_scr[r0:r0 + WINDOW, c0:c0 + LANES] = o[0:WINDOW, :].astype(BF16)
        attn_scr[r0:r0 + WINDOW, c0 + LANES:c0 + 2 * LANES] = o[WINDOW:2 * WINDOW, :].astype(BF16)

    items = [(qb, g) for qb in range(tm // WINDOW) for g in range(N_KV)]
    batches = [items[i:i + ATTN_BATCH] for i in range(0, len(items), ATTN_BATCH)]
    s_next = [scores(*it) for it in batches[0]]
    for k, batch in enumerate(batches):
        s_cur = s_next
        if k + 1 < len(batches):
            s_next = [scores(*it) for it in batches[k + 1]]
        for it, s in zip(batch, s_cur):
            attend(it, s)

    kbd_scr[:, :, 0:WINDOW, :] = kbd_scr[:, :, tm:tm + WINDOW, :]
    vbd_scr[:, :, 0:WINDOW, :] = vbd_scr[:, :, tm:tm + WINDOW, :]
    branch_a = _dot(attn_scr[...], w_o_ref[...])

    n_tiles = tm // SUBLANES
    assert n_tiles >= CONV_W - 1, "a tap must not reach further back than the previous-sublane tiles"
    xn_scr[...] = xn_f32.reshape(SUBLANES, n_tiles, D)
    xn_perm = jnp.concatenate([xn_scr[:, p, :] for p in range(n_tiles)], axis=0).astype(BF16)
    glu = _dot(xn_perm, w_in_ref[:, GLU_OFF:GATE_OFF])
    u_perm = (glu[:, 0:D] * _sigmoid(glu[:, D:2 * D])).reshape(n_tiles, SUBLANES, D)
    ext_scr[n_tiles:2 * n_tiles] = u_perm
    prev_last = jnp.broadcast_to(carry_scr[...][:, None, :], (n_tiles, SUBLANES, D))
    first_sub = lax.broadcasted_iota(jnp.int32, (n_tiles, SUBLANES, D), 1) == 0
    ext_scr[0:n_tiles] = jnp.where(first_sub, prev_last, pltpu.roll(u_perm, 1, axis=1))

    @pl.when(t == last)
    def _emit_conv_state():
        cbs_ref[0] = ext_scr[2 * n_tiles - (CONV_W - 1):2 * n_tiles, SUBLANES - 1, :]

    carry_scr[...] = ext_scr[n_tiles:2 * n_tiles, SUBLANES - 1, :]

    tiles = n_tiles // CONV_STEPS
    lane_chunk = 128

    def _conv_rows(i, carry):
        p0 = i * tiles
        for lc in range(D // lane_chunk):
            sl = slice(lc * lane_chunk, (lc + 1) * lane_chunk)
            acc = jnp.broadcast_to(cb_ref[:, sl], (tiles * SUBLANES, lane_chunk))
            for j in range(CONV_W):
                back = CONV_W - 1 - j
                w = jnp.concatenate([cw_ref[j, :, sl]] * tiles, axis=0)
                win = ext_scr[pl.ds(n_tiles + p0 - back, tiles), :, sl]
                acc = acc + win.reshape(tiles * SUBLANES, lane_chunk) * w
            c_scr[pl.ds(p0, tiles), :, sl] = acc.reshape(tiles, SUBLANES, lane_chunk)
        gates_scr[i] = _dot(xnb_scr[...], w_gate_ref[i])
        return carry

    lax.fori_loop(0, CONV_STEPS, _conv_rows, 0)

    y = _layernorm(c_scr[...].reshape(tm, D), lng_ref[...], lnb_ref[...])
    c_scr[...] = _dot((y * _sigmoid(y)).astype(BF16), w_pb_ref[...]).reshape(n_tiles, SUBLANES, D)
    branch_b = jnp.concatenate(
        [c_scr[(n * SUBLANES) % n_tiles:(n * SUBLANES) % n_tiles + SUBLANES, (n * SUBLANES) // n_tiles, :]
         for n in range(n_tiles)], axis=0)

    gates = _sigmoid(jnp.concatenate([gates_scr[i] for i in range(CONV_STEPS)], axis=1))
    merged = gates[:, 0:D] * branch_a + gates[:, D:2 * D] * branch_b
    h_ref[0] = x + _dot(merged.astype(BF16), w_out_ref[...])


def _prompt_stage_a(x, g1, w_in, w_gate, sinks, w_o, cw, cb, lng, lnb, w_pb, w_out, *, tm):
    b, s, _ = x.shape
    grid = (b, s // tm)
    row_block = pl.BlockSpec((1, tm, D), lambda i, t: (i, t, 0))
    per_seq = lambda r, c: pl.BlockSpec((1, r, c), lambda i, t: (i, 0, 0))
    return pl.pallas_call(
        functools.partial(_prompt_a_body, tm=tm),
        grid=grid,
        in_specs=[row_block, _resident((1, D)), _resident((D, GATE_OFF)),
                  _resident((CONV_STEPS, D, 2 * D // CONV_STEPS)),
                  pl.BlockSpec(memory_space=pltpu.SMEM),
                  _resident((D, D)), _resident((CONV_W, SUBLANES, D)), _resident((1, D)), _resident((1, D)),
                  _resident((1, D)), _resident((D, D)), _resident((D, D))],
        out_specs=[row_block, per_seq(WINDOW, KV_DIM), per_seq(WINDOW, KV_DIM), per_seq(CONV_W - 1, D)],
        out_shape=[jax.ShapeDtypeStruct((b, s, D), F32),
                   jax.ShapeDtypeStruct((b, WINDOW, KV_DIM), F32),
                   jax.ShapeDtypeStruct((b, WINDOW, KV_DIM), F32),
                   jax.ShapeDtypeStruct((b, CONV_W - 1, D), F32)],
        scratch_shapes=[pltpu.VMEM((N_KV, 2, WINDOW + tm, LANES), BF16),
                        pltpu.VMEM((N_KV, 2, WINDOW + tm, LANES), BF16),
                        pltpu.VMEM((tm, D), BF16),
                        pltpu.VMEM((tm, D), BF16),
                        pltpu.VMEM((SUBLANES, tm // SUBLANES, D), F32),
                        pltpu.VMEM((2 * tm // SUBLANES, SUBLANES, D), F32),
                        pltpu.VMEM((tm // SUBLANES, D), F32),
                        pltpu.VMEM((tm // SUBLANES, SUBLANES, D), F32),
                        pltpu.VMEM((tm, D), BF16),
                        pltpu.VMEM((CONV_STEPS, tm, 2 * D // CONV_STEPS), F32)],
        compiler_params=pltpu.CompilerParams(dimension_semantics=("arbitrary", "arbitrary"),
                                             vmem_limit_bytes=VMEM_LIMIT),
        name="prompt_stage_a",
    )(x, g1, w_in, w_gate, sinks, w_o, cw, cb, lng, lnb, w_pb, w_out)


def _ffn_chunks(h, hn, prev_rows, w_up_ref, fcw_ref, fcb_ref, wd_ref):
    def up_chunk(c):
        halves = []
        for part in range(2):
            cols = slice(part * D_FF + c * FFN_CH, part * D_FF + (c + 1) * FFN_CH)
            up = _dot(hn, w_up_ref[:, cols])
            prev1, prev2 = prev_rows(cols, up)
            w = fcw_ref[:, cols]
            halves.append(w[0:1, :] * prev2 + w[1:2, :] * prev1 + w[2:3, :] * up + fcb_ref[:, cols])
        return halves

    acc = h
    ahead = [up_chunk(c) for c in range(min(FFN_LOOKAHEAD, FFN_NC))]
    for c in range(FFN_NC):
        gate, val = ahead.pop(0)
        if c + FFN_LOOKAHEAD < FFN_NC:
            ahead.append(up_chunk(c + FFN_LOOKAHEAD))
        act = (gate * _sigmoid(gate) * val).astype(BF16)
        acc = acc + _dot(act, wd_ref[c * FFN_CH:(c + 1) * FFN_CH, :])
    return acc


def _prompt_ffn_body(h_ref, g2_ref, w_up_ref, fcw_ref, fcb_ref, wd_ref, gf_ref,
                     y_ref, state_ref, carry_scr, *, tm):
    t = pl.program_id(1)

    @pl.when(t == 0)
    def _zero_history():
        carry_scr[...] = jnp.zeros(carry_scr.shape, F32)

    h = h_ref[0]
    hn = _rmsnorm(h, g2_ref[...]).astype(BF16)

    def prev_rows(cols, up):
        ext = jnp.concatenate([carry_scr[:, cols], up], axis=0)
        carry_scr[:, cols] = up[tm - SUBLANES:, :]
        state_ref[0, :, cols] = up[tm - (FFN_CONV_W - 1):, :]
        return ext[SUBLANES - 1:SUBLANES - 1 + tm, :], ext[SUBLANES - 2:SUBLANES - 2 + tm, :]

    y = _ffn_chunks(h, hn, prev_rows, w_up_ref, fcw_ref, fcb_ref, wd_ref)
    y_ref[0] = _rmsnorm(y, gf_ref[...])


def _ffn_weight_specs():
    return [_resident((1, D)), _resident((D, 2 * D_FF)), _resident((FFN_CONV_W, 2 * D_FF)),
            _resident((1, 2 * D_FF)), _resident((D_FF, D)), _resident((1, D))]


def _prompt_ffn(h, ffn_w, *, tm):
    b, s, _ = h.shape
    row_block = pl.BlockSpec((1, tm, D), lambda i, t: (i, t, 0))
    state_shape = (b, FFN_CONV_W - 1, 2 * D_FF)
    return pl.pallas_call(
        functools.partial(_prompt_ffn_body, tm=tm),
        grid=(b, s // tm),
        in_specs=[row_block] + _ffn_weight_specs(),
        out_specs=[row_block, pl.BlockSpec((1,) + state_shape[1:], lambda i, t: (i, 0, 0))],
        out_shape=[jax.ShapeDtypeStruct((b, s, D), F32), jax.ShapeDtypeStruct(state_shape, F32)],
        scratch_shapes=[pltpu.VMEM((SUBLANES, 2 * D_FF), F32)],
        compiler_params=pltpu.CompilerParams(dimension_semantics=("arbitrary", "arbitrary"),
                                             vmem_limit_bytes=VMEM_LIMIT),
        name="prompt_ffn",
    )(h, *ffn_w)


def _sample_proj_body(x_ref, g1_ref, w_in_ref, q_ref, kvt_ref, u_ref, gates_ref):
    xn = _rmsnorm(x_ref[...], g1_ref[...]).astype(BF16)
    q_ref[...] = _dot(xn, w_in_ref[:, 0:D]) * SCALE
    kvt_ref[...] = lax.dot_general(w_in_ref[:, D:QKV_W], xn, (((0,), (1,)), ((), ())),
                                   preferred_element_type=F32)
    glu = _dot(xn, w_in_ref[:, GLU_OFF:GATE_OFF])
    u_ref[...] = glu[:, 0:D] * _sigmoid(glu[:, D:2 * D])
    gates_ref[...] = _sigmoid(_dot(xn, w_in_ref[:, GATE_OFF:IN_DIM]))


def _sample_proj(x2d, g1, w_in):
    m = x2d.shape[0]
    full = lambda r, c: pl.BlockSpec((r, c), lambda i: (0, 0))
    return pl.pallas_call(
        _sample_proj_body,
        grid=(1,),
        in_specs=[full(m, D), _resident((1, D)), _resident((D, IN_DIM))],
        out_specs=[full(m, D), full(2 * KV_DIM, m), full(m, D), full(m, 2 * D)],
        out_shape=[jax.ShapeDtypeStruct((m, D), F32), jax.ShapeDtypeStruct((2 * KV_DIM, m), F32),
                   jax.ShapeDtypeStruct((m, D), F32), jax.ShapeDtypeStruct((m, 2 * D), F32)],
        compiler_params=pltpu.CompilerParams(dimension_semantics=("arbitrary",),
                                             vmem_limit_bytes=VMEM_LIMIT),
        name="sample_proj",
    )(x2d, g1, w_in)


def _sample_mix_body(q_ref, kvt_ref, u_ref, kt_ref, vt_ref, cstate_ref, sink_ref,
                     cw_ref, cb_ref, lng_ref, lnb_ref,
                     attn_ref, cs_ref, kt_out_ref, vt_out_ref, cstate_out_ref,
                     qg_scr, *, gs, n_tok):
    rows_per_group = GROUP * n_tok
    keep = WINDOW - n_tok

    lane = lax.broadcasted_iota(jnp.int32, (N_KV, HEAD_DIM, WINDOW), 2)
    for n in range(gs):
        shift = (keep - n_tok * n) % WINDOW
        new_k = pltpu.roll(kvt_ref[0:KV_DIM, :], shift, axis=1).reshape(N_KV, HEAD_DIM, WINDOW)
        new_v = pltpu.roll(kvt_ref[KV_DIM:2 * KV_DIM, :], shift, axis=1).reshape(N_KV, HEAD_DIM, WINDOW)
        kt_out_ref[n] = jnp.where(lane >= keep, new_k, pltpu.roll(kt_ref[n], keep, axis=2))
        vt_out_ref[n] = jnp.where(lane >= keep, new_v, pltpu.roll(vt_ref[n], keep, axis=2))

    tok = lax.broadcasted_iota(jnp.int32, (rows_per_group, WINDOW), 0) % n_tok
    key = lax.broadcasted_iota(jnp.int32, (rows_per_group, WINDOW), 1)
    mask_old = (key > tok)[None]
    mask_new = jnp.logical_and(key >= keep, key - keep <= tok)[None]
    for g in range(N_KV):
        for hh in range(GROUP):
            h0 = (GROUP * g + hh) * HEAD_DIM
            qg_scr[:, hh * n_tok:(hh + 1) * n_tok, :] = q_ref[:, h0:h0 + HEAD_DIM].reshape(gs, n_tok, HEAD_DIM)
        qg = qg_scr[...].astype(BF16)
        s_old = jnp.einsum("nqd,ndk->nqk", qg, kt_ref[:, g].astype(BF16), preferred_element_type=F32)
        s_new = jnp.einsum("nqd,ndk->nqk", qg, kt_out_ref[:, g].astype(BF16), preferred_element_type=F32)
        s_old = jnp.where(mask_old, s_old, NEG_INF)
        s_new = jnp.where(mask_new, s_new, NEG_INF)
        sink = sink_ref[g][None]
        m = jnp.maximum(jnp.maximum(jnp.max(s_old, axis=-1, keepdims=True),
                                    jnp.max(s_new, axis=-1, keepdims=True)), sink)
        e_old = jnp.exp(s_old - m)
        e_new = jnp.exp(s_new - m)
        den = (jnp.sum(e_old, axis=-1, keepdims=True) + jnp.sum(e_new, axis=-1, keepdims=True)
               + jnp.exp(sink - m))
        o = (jnp.einsum("nqk,ndk->nqd", (e_old / den).astype(BF16), vt_ref[:, g].astype(BF16),
                        preferred_element_type=F32)
             + jnp.einsum("nqk,ndk->nqd", (e_new / den).astype(BF16), vt_out_ref[:, g].astype(BF16),
                          preferred_element_type=F32))
        for hh in range(GROUP):
            h0 = (GROUP * g + hh) * HEAD_DIM
            attn_ref[:, h0:h0 + HEAD_DIM] = o[:, hh * n_tok:(hh + 1) * n_tok, :].reshape(gs * n_tok, HEAD_DIM)

    n_state = CONV_W - 1
    u_slabs = [u_ref[:, i, :] for i in range(n_tok)]

    def ext(r, sl):
        return cstate_ref[r, :, sl] if r < n_state else u_slabs[r - n_state][:, sl]

    for r in range(n_state):
        cstate_out_ref[r] = ext(r + n_tok, slice(None))

    lane_chunk = 256
    for i in range(n_tok):
        pieces = []
        for lc in range(D // lane_chunk):
            sl = slice(lc * lane_chunk, (lc + 1) * lane_chunk)
            acc = jnp.broadcast_to(cb_ref[:, sl], (gs, lane_chunk))
            for j in range(CONV_W):
                w = jnp.concatenate([cw_ref[j, :, sl]] * (gs // SUBLANES), axis=0)
                acc = acc + ext(i + j, sl) * w
            pieces.append(acc)
        y = _layernorm(jnp.concatenate(pieces, axis=1), lng_ref[...], lnb_ref[...])
        cs_ref[:, i, :] = y * _sigmoid(y)


def _sample_mix(q, kvt, u, kt, vt, cstate, sink_rows, cw, cb, lng, lnb, *, gs):
    n_seq, n_tok, _ = u.shape
    rows = gs * n_tok
    assert rows % LANES == 0, "a group's token columns must fill whole lane tiles of the transposed k/v"
    row_block = pl.BlockSpec((rows, D), lambda s: (s, 0))
    seq_block = pl.BlockSpec((gs, n_tok, D), lambda s: (s, 0, 0))
    cache_block = pl.BlockSpec((gs, N_KV, HEAD_DIM, WINDOW), lambda s: (s, 0, 0, 0))
    state_block = pl.BlockSpec((CONV_W - 1, gs, D), lambda s: (0, s, 0))
    rows_per_group = GROUP * n_tok
    return pl.pallas_call(
        functools.partial(_sample_mix_body, gs=gs, n_tok=n_tok),
        grid=(n_seq // gs,),
        in_specs=[row_block, pl.BlockSpec((2 * KV_DIM, rows), lambda s: (0, s)), seq_block,
                  cache_block, cache_block, state_block,
                  _resident((N_KV, rows_per_group, 1)),
                  _resident((CONV_W, SUBLANES, D)), _resident((1, D)), _resident((1, D)), _resident((1, D))],
        out_specs=[row_block, seq_block, cache_block, cache_block, state_block],
        out_shape=[jax.ShapeDtypeStruct((n_seq * n_tok, D), F32), jax.ShapeDtypeStruct((n_seq, n_tok, D), F32),
                   jax.ShapeDtypeStruct((n_seq, N_KV, HEAD_DIM, WINDOW), F32),
                   jax.ShapeDtypeStruct((n_seq, N_KV, HEAD_DIM, WINDOW), F32),
                   jax.ShapeDtypeStruct((CONV_W - 1, n_seq, D), F32)],
        scratch_shapes=[pltpu.VMEM((gs, rows_per_group, HEAD_DIM), F32)],
        compiler_params=pltpu.CompilerParams(dimension_semantics=("arbitrary",),
                                             vmem_limit_bytes=VMEM_LIMIT),
        name="sample_mix",
    )(q, kvt, u, kt, vt, cstate, sink_rows, cw, cb, lng, lnb)


def _sample_merge_body(x_ref, attn_ref, cs_ref, gates_ref, w_o_ref, w_pb_ref, w_out_ref, h_ref):
    x = x_ref[...]
    branch_a = _dot(attn_ref[...].astype(BF16), w_o_ref[...])
    branch_b = _dot(cs_ref[...].astype(BF16), w_pb_ref[...])
    merged = gates_ref[:, 0:D] * branch_a + gates_ref[:, D:2 * D] * branch_b
    h_ref[...] = x + _dot(merged.astype(BF16), w_out_ref[...])


def _sample_merge(x, attn, cs, gates, w_o, w_pb, w_out):
    m = x.shape[0]
    full = lambda c: pl.BlockSpec((m, c), lambda i: (0, 0))
    return pl.pallas_call(
        _sample_merge_body,
        grid=(1,),
        in_specs=[full(D), full(D), full(D), full(2 * D),
                  _resident((D, D)), _resident((D, D)), _resident((D, D))],
        out_specs=full(D),
        out_shape=jax.ShapeDtypeStruct((m, D), F32),
        compiler_params=pltpu.CompilerParams(dimension_semantics=("arbitrary",),
                                             vmem_limit_bytes=VMEM_LIMIT),
        name="sample_merge",
    )(x, attn, cs, gates, w_o, w_pb, w_out)


def _sample_ffn_body(h_ref, state_ref, g2_ref, w_up_ref, fcw_ref, fcb_ref, wd_ref, gf_ref,
                     y_ref, state_out_ref, *, n_seq, n_tok):
    m = n_seq * n_tok
    h = jnp.concatenate([h_ref[:, i, :] for i in range(n_tok)], axis=0)
    hn = _rmsnorm(h, g2_ref[...]).astype(BF16)

    def prev_rows(cols, up):
        s0 = state_ref[0, :, cols]
        s1 = state_ref[1, :, cols]
        state_out_ref[0, :, cols] = up[m - 2 * n_seq:m - n_seq, :]
        state_out_ref[1, :, cols] = up[m - n_seq:, :]
        prev1 = jnp.concatenate([s1, up[:m - n_seq, :]], axis=0)
        prev2 = jnp.concatenate([s0, s1, up[:m - 2 * n_seq, :]], axis=0)
        return prev1, prev2

    y = _ffn_chunks(h, hn, prev_rows, w_up_ref, fcw_ref, fcb_ref, wd_ref)
    y = _rmsnorm(y, gf_ref[...])
    for i in range(n_tok):
        y_ref[:, i, :] = y[i * n_seq:(i + 1) * n_seq, :]


def _sample_ffn(h, state, ffn_w, *, n_seq, n_tok):
    m = n_seq * n_tok
    state_shape = (FFN_CONV_W - 1, n_seq, 2 * D_FF)
    state_spec = pl.BlockSpec(state_shape, lambda i: (0, 0, 0))
    seq_major = pl.BlockSpec((n_seq, n_tok, D), lambda i: (0, 0, 0))
    return pl.pallas_call(
        functools.partial(_sample_ffn_body, n_seq=n_seq, n_tok=n_tok),
        grid=(1,),
        in_specs=[seq_major, state_spec] + _ffn_weight_specs(),
        out_specs=[seq_major, state_spec],
        out_shape=[jax.ShapeDtypeStruct((n_seq, n_tok, D), F32), jax.ShapeDtypeStruct(state_shape, F32)],
        compiler_params=pltpu.CompilerParams(dimension_semantics=("arbitrary",),
                                             vmem_limit_bytes=VMEM_LIMIT),
        name="sample_ffn",
    )(h, state, *ffn_w)


def _layer(x_prompt, x_sample, k_past, v_past, conv_b_past, conv_f_past,
           norm1_g, w_in, sinks, w_o_attn, conv_b_w, conv_b_bias, ln_b_g, ln_b_b, w_pb, w_out,
           norm2_g, w_up, ffn_conv_w, ffn_conv_b, w_down, norm_f_g, *, tm_a, tm_ffn, gs):
    n_seq, n_tok, _ = x_sample.shape
    row = lambda v: v.reshape(1, -1)
    g1, g2, gf = row(norm1_g), row(norm2_g), row(norm_f_g)
    cb, lng, lnb = row(conv_b_bias), row(ln_b_g), row(ln_b_b)
    w_in_b, w_o_b, w_pb_b, w_out_b = (w.astype(BF16) for w in (w_in, w_o_attn, w_pb, w_out))
    ffn_w = (g2, w_up.astype(BF16), ffn_conv_w, row(ffn_conv_b), w_down.astype(BF16), gf)

    cw8 = jnp.broadcast_to(conv_b_w[:, None, :], (CONV_W, SUBLANES, D))
    h_p, k_win_p, v_win_p, conv_b_p = _prompt_stage_a(
        x_prompt, g1, w_in_b[:, :GATE_OFF],
        w_in_b[:, GATE_OFF:].reshape(D, CONV_STEPS, 2 * D // CONV_STEPS).transpose(1, 0, 2),
        sinks, w_o_b, cw8, cb, lng, lnb, w_pb_b, w_out_b, tm=tm_a)
    y_p, conv_f_p = _prompt_ffn(h_p, ffn_w, tm=tm_ffn)

    m = n_seq * n_tok
    q, kvt, u, gates = _sample_proj(x_sample.reshape(m, D), g1, w_in_b)
    sink_rows = jnp.repeat(sinks.reshape(N_KV, GROUP), n_tok, axis=1)[:, :, None]
    attn, cs, kt_s, vt_s, cstate_s = _sample_mix(
        q, kvt, u.reshape(n_seq, n_tok, D),
        k_past.transpose(0, 2, 3, 1), v_past.transpose(0, 2, 3, 1), conv_b_past.transpose(1, 0, 2),
        sink_rows, cw8, cb, lng, lnb, gs=gs)
    conv_b_s = cstate_s.transpose(1, 0, 2)
    h_s = _sample_merge(x_sample.reshape(m, D), attn, cs.reshape(m, D), gates, w_o_b, w_pb_b, w_out_b)
    y_s, ffn_state_s = _sample_ffn(h_s.reshape(n_seq, n_tok, D), conv_f_past.transpose(1, 0, 2), ffn_w,
                                   n_seq=n_seq, n_tok=n_tok)
    conv_f_s = ffn_state_s.transpose(1, 0, 2)

    kv_shape = lambda a: a.reshape(a.shape[0], WINDOW, N_KV, HEAD_DIM)
    return (y_p, y_s,
            kv_shape(k_win_p), kv_shape(v_win_p), conv_b_p, conv_f_p,
            kt_s.transpose(0, 3, 1, 2), vt_s.transpose(0, 3, 1, 2), conv_b_s, conv_f_s)


def kernel(x_prompt, x_sample, cache_k_win, cache_v_win, state_conv_b, state_conv_ffn, norm1_g, w_in, sinks,
           w_o_attn, conv_b_w, conv_b_bias, ln_b_g, ln_b_b, w_pb, w_out, norm2_g, w_up, ffn_conv_w,
           ffn_conv_b, w_down, norm_f_g):
    assert w_in.shape[0] == 1, "single-layer stack only"
    res = _layer(x_prompt, x_sample, cache_k_win[0], cache_v_win[0], state_conv_b[0], state_conv_ffn[0],
                 norm1_g[0], w_in[0], sinks[0], w_o_attn[0], conv_b_w[0], conv_b_bias[0], ln_b_g[0],
                 ln_b_b[0], w_pb[0], w_out[0], norm2_g[0], w_up[0], ffn_conv_w[0], ffn_conv_b[0],
                 w_down[0], norm_f_g, tm_a=256, tm_ffn=512, gs=16)
    return res[:2] + tuple(r[None] for r in res[2:])
```

```python
import functools

import jax
import jax.numpy as jnp
from jax import lax
from jax.experimental import pallas as pl
from jax.experimental.pallas import tpu as pltpu

F32 = jnp.float32
BF16 = jnp.bfloat16

D = 1024
N_HEADS, N_KV, HEAD_DIM = 16, 4, 64
WINDOW = 128
KV_DIM = N_KV * HEAD_DIM
CONV_W = 31
D_FF = 2816
FFN_CONV_W = 3
EPS = 1e-6
NEG_INF = -1e30
SCALE = HEAD_DIM ** -0.5
QKV_W = D + 2 * KV_DIM
GLU_OFF = QKV_W
GATE_OFF = QKV_W + 2 * D
IN_DIM = QKV_W + 4 * D
LANES = 128
SUBLANES = 8
GROUP = N_HEADS // N_KV
ATTN_BATCH = 4
CONV_TILES = 8
FFN_CH = 256
FFN_NC = D_FF // FFN_CH
VMEM_LIMIT = 56 * 1024 * 1024


def _resident(shape):
    nd = len(shape)
    return pl.BlockSpec(shape, lambda *_: (0,) * nd, pipeline_mode=pl.Buffered(1))


def _sigmoid(x):
    return 1.0 / (1.0 + jnp.exp(-x))


def _rmsnorm(x, g):
    return x * lax.rsqrt(jnp.mean(x * x, axis=-1, keepdims=True) + EPS) * g


def _layernorm(x, g, b):
    mu = jnp.mean(x, axis=-1, keepdims=True)
    xc = x - mu
    var = jnp.mean(xc * xc, axis=-1, keepdims=True)
    return xc * lax.rsqrt(var + EPS) * g + b


def _dot(a, b):
    return jnp.dot(a, b, preferred_element_type=F32)


def _softmax_with_sink(s, mask, sink):
    s = jnp.where(mask, s, NEG_INF)
    m = jnp.maximum(jnp.max(s, axis=-1, keepdims=True), sink)
    e = jnp.exp(s - m)
    den = jnp.sum(e, axis=-1, keepdims=True) + jnp.exp(sink - m)
    return e / den


def _split_heads_block_diag(x, lo):
    xr = pltpu.roll(x, HEAD_DIM, axis=1)
    z = jnp.zeros_like(x)
    even = (jnp.where(lo, x, z), jnp.where(lo, z, xr))
    odd = (jnp.where(lo, xr, z), jnp.where(lo, z, x))
    return even, odd


def _prompt_a_body(x_ref, g1_ref, w_in_ref, sinks_ref, w_o_ref, cw_ref, cb_ref, lng_ref, lnb_ref,
                   w_pb_ref, w_out_ref,
                   h_ref, kwin_ref, vwin_ref, cbs_ref,
                   kbd_scr, vbd_scr, q_scr, attn_scr, xn_scr, ext_scr, carry_scr, c_scr, *, tm):
    t = pl.program_id(1)
    last = pl.num_programs(1) - 1

    @pl.when(t == 0)
    def _zero_history():
        kbd_scr[:, :, 0:WINDOW, :] = jnp.zeros((N_KV, 2, WINDOW, LANES), BF16)
        vbd_scr[:, :, 0:WINDOW, :] = jnp.zeros((N_KV, 2, WINDOW, LANES), BF16)
        carry_scr[...] = jnp.zeros(carry_scr.shape, F32)

    x = x_ref[0]
    xn_f32 = _rmsnorm(x, g1_ref[...])
    xn = xn_f32.astype(BF16)

    qkv = _dot(xn, w_in_ref[:, 0:QKV_W])
    q_scr[...] = qkv[:, 0:D].astype(BF16)
    k_new = qkv[:, D:D + KV_DIM]
    v_new = qkv[:, D + KV_DIM:QKV_W]

    @pl.when(t == last)
    def _emit_kv_window():
        kwin_ref[0] = k_new[tm - WINDOW:, :]
        vwin_ref[0] = v_new[tm - WINDOW:, :]

    lo = lax.broadcasted_iota(jnp.int32, (tm, LANES), 1) < HEAD_DIM
    for m in range(N_KV // 2):
        k_even, k_odd = _split_heads_block_diag(k_new[:, LANES * m:LANES * (m + 1)] * SCALE, lo)
        v_even, v_odd = _split_heads_block_diag(v_new[:, LANES * m:LANES * (m + 1)], lo)
        for half in range(2):
            kbd_scr[2 * m, half, WINDOW:, :] = k_even[half].astype(BF16)
            kbd_scr[2 * m + 1, half, WINDOW:, :] = k_odd[half].astype(BF16)
            vbd_scr[2 * m, half, WINDOW:, :] = v_even[half].astype(BF16)
            vbd_scr[2 * m + 1, half, WINDOW:, :] = v_odd[half].astype(BF16)

    row = lax.broadcasted_iota(jnp.int32, (WINDOW, 2 * WINDOW), 0)
    col = lax.broadcasted_iota(jnp.int32, (WINDOW, 2 * WINDOW), 1)
    band = jnp.logical_and(col > row, col <= row + WINDOW)
    first_mask = jnp.logical_and(band, col >= jnp.where(t > 0, 0, WINDOW))

    def scores(qb, g):
        r0 = qb * WINDOW
        c0 = 2 * LANES * g
        q_pairs = jnp.concatenate([q_scr[r0:r0 + WINDOW, c0:c0 + LANES],
                                   q_scr[r0:r0 + WINDOW, c0 + LANES:c0 + 2 * LANES]], axis=0)
        kbd = jnp.concatenate([kbd_scr[g, 0, r0:r0 + 2 * WINDOW, :],
                               kbd_scr[g, 1, r0:r0 + 2 * WINDOW, :]], axis=0)
        return lax.dot_general(q_pairs, kbd, (((1,), (1,)), ((), ())), preferred_element_type=F32)

    def attend(item, s):
        qb, g = item
        r0 = qb * WINDOW
        c0 = 2 * LANES * g
        mask = first_mask if qb == 0 else band
        probs = []
        for pair in range(2):
            halves = []
            for hh in range(2):
                sink = sinks_ref[GROUP * g + 2 * pair + hh]
                p = _softmax_with_sink(s[pair * WINDOW:(pair + 1) * WINDOW, 2 * WINDOW * hh:2 * WINDOW * (hh + 1)],
                                       mask, sink)
                halves.append(p.astype(BF16))
            probs.append(jnp.concatenate(halves, axis=1))
        vbd = jnp.concatenate([vbd_scr[g, 0, r0:r0 + 2 * WINDOW, :],
                               vbd_scr[g, 1, r0:r0 + 2 * WINDOW, :]], axis=0)
        o = _dot(jnp.concatenate(probs, axis=0), vbd)
        attn_scr[r0:r0 + WINDOW, c0:c0 + LANES] = o[0:WINDOW, :].astype(BF16)
        attn_scr[r0:r0 + WINDOW, c0 + LANES:c0 + 2 * LANES] = o[WINDOW:2 * WINDOW, :].astype(BF16)

    items = [(qb, g) for qb in range(tm // WINDOW) for g in range(N_KV)]
    batches = [items[i:i + ATTN_BATCH] for i in range(0, len(items), ATTN_BATCH)]
    s_next = [scores(*it) for it in batches[0]]
    for k, batch in enumerate(batches):
        s_cur = s_next
        if k + 1 < len(batches):
            s_next = [scores(*it) for it in batches[k + 1]]
        for it, s in zip(batch, s_cur):
            attend(it, s)

    kbd_scr[:, :, 0:WINDOW, :] = kbd_scr[:, :, tm:tm + WINDOW, :]
    vbd_scr[:, :, 0:WINDOW, :] = vbd_scr[:, :, tm:tm + WINDOW, :]
    branch_a = _dot(attn_scr[...], w_o_ref[...])

    n_tiles = tm // SUBLANES
    assert n_tiles >= CONV_W - 1, "a tap must not reach further back than the previous-sublane tiles"
    xn_scr[...] = xn_f32.reshape(SUBLANES, n_tiles, D)
    xn_perm = jnp.concatenate([xn_scr[:, p, :] for p in range(n_tiles)], axis=0).astype(BF16)
    glu = _dot(xn_perm, w_in_ref[:, GLU_OFF:GATE_OFF])
    u_perm = (glu[:, 0:D] * _sigmoid(glu[:, D:2 * D])).reshape(n_tiles, SUBLANES, D)
    ext_scr[n_tiles:2 * n_tiles] = u_perm
    prev_last = jnp.broadcast_to(carry_scr[...][:, None, :], (n_tiles, SUBLANES, D))
    first_sub = lax.broadcasted_iota(jnp.int32, (n_tiles, SUBLANES, D), 1) == 0
    ext_scr[0:n_tiles] = jnp.where(first_sub, prev_last, pltpu.roll(u_perm, 1, axis=1))

    @pl.when(t == last)
    def _emit_conv_state():
        cbs_ref[0] = ext_scr[2 * n_tiles - (CONV_W - 1):2 * n_tiles, SUBLANES - 1, :]

    carry_scr[...] = ext_scr[n_tiles:2 * n_tiles, SUBLANES - 1, :]

    tiles = CONV_TILES
    lane_chunk = 128

    def _conv_rows(i, carry):
        p0 = i * tiles
        for lc in range(D // lane_chunk):
            sl = slice(lc * lane_chunk, (lc + 1) * lane_chunk)
            acc = jnp.broadcast_to(cb_ref[:, sl], (tiles * SUBLANES, lane_chunk))
            for j in range(CONV_W):
                back = CONV_W - 1 - j
                w = jnp.concatenate([cw_ref[j, :, sl]] * tiles, axis=0)
                win = ext_scr[pl.ds(n_tiles + p0 - back, tiles), :, sl]
                acc = acc + win.reshape(tiles * SUBLANES, lane_chunk) * w
            c_scr[pl.ds(p0, tiles), :, sl] = acc.reshape(tiles, SUBLANES, lane_chunk)
        return carry

    lax.fori_loop(0, n_tiles // tiles, _conv_rows, 0)

    y = _layernorm(c_scr[...].reshape(tm, D), lng_ref[...], lnb_ref[...])
    c_scr[...] = _dot((y * _sigmoid(y)).astype(BF16), w_pb_ref[...]).reshape(n_tiles, SUBLANES, D)
    branch_b = jnp.concatenate(
        [c_scr[(n * SUBLANES) % n_tiles:(n * SUBLANES) % n_tiles + SUBLANES, (n * SUBLANES) // n_tiles, :]
         for n in range(n_tiles)], axis=0)

    gates = _dot(xn, w_in_ref[:, GATE_OFF:IN_DIM])
    merged = _sigmoid(gates[:, 0:D]) * branch_a + _sigmoid(gates[:, D:2 * D]) * branch_b
    h_ref[0] = x + _dot(merged.astype(BF16), w_out_ref[...])


def _prompt_stage_a(x, g1, w_in, sinks, w_o, cw, cb, lng, lnb, w_pb, w_out, *, tm):
    b, s, _ = x.shape
    grid = (b, s // tm)
    row_block = pl.BlockSpec((1, tm, D), lambda i, t: (i, t, 0))
    per_seq = lambda r, c: pl.BlockSpec((1, r, c), lambda i, t: (i, 0, 0))
    return pl.pallas_call(
        functools.partial(_prompt_a_body, tm=tm),
        grid=grid,
        in_specs=[row_block, _resident((1, D)), _resident((D, IN_DIM)),
                  pl.BlockSpec(memory_space=pltpu.SMEM),
                  _resident((D, D)), _resident((CONV_W, SUBLANES, D)), _resident((1, D)), _resident((1, D)),
                  _resident((1, D)), _resident((D, D)), _resident((D, D))],
        out_specs=[row_block, per_seq(WINDOW, KV_DIM), per_seq(WINDOW, KV_DIM), per_seq(CONV_W - 1, D)],
        out_shape=[jax.ShapeDtypeStruct((b, s, D), F32),
                   jax.ShapeDtypeStruct((b, WINDOW, KV_DIM), F32),
                   jax.ShapeDtypeStruct((b, WINDOW, KV_DIM), F32),
                   jax.ShapeDtypeStruct((b, CONV_W - 1, D), F32)],
        scratch_shapes=[pltpu.VMEM((N_KV, 2, WINDOW + tm, LANES), BF16),
                        pltpu.VMEM((N_KV, 2, WINDOW + tm, LANES), BF16),
                        pltpu.VMEM((tm, D), BF16),
                        pltpu.VMEM((tm, D), BF16),
                        pltpu.VMEM((SUBLANES, tm // SUBLANES, D), F32),
                        pltpu.VMEM((2 * tm // SUBLANES, SUBLANES, D), F32),
                        pltpu.VMEM((tm // SUBLANES, D), F32),
                        pltpu.VMEM((tm // SUBLANES, SUBLANES, D), F32)],
        compiler_params=pltpu.CompilerParams(dimension_semantics=("arbitrary", "arbitrary"),
                                             vmem_limit_bytes=VMEM_LIMIT),
        name="prompt_stage_a",
    )(x, g1, w_in, sinks, w_o, cw, cb, lng, lnb, w_pb, w_out)


def _ffn_chunks(h, hn, prev_rows, w_up_ref, fcw_ref, fcb_ref, wd_ref):
    def up_chunk(c):
        halves = []
        for part in range(2):
            cols = slice(part * D_FF + c * FFN_CH, part * D_FF + (c + 1) * FFN_CH)
            up = _dot(hn, w_up_ref[:, cols])
            prev1, prev2 = prev_rows(cols, up)
            w = fcw_ref[:, cols]
            halves.append(w[0:1, :] * prev2 + w[1:2, :] * prev1 + w[2:3, :] * up + fcb_ref[:, cols])
        return halves

    acts = []
    for c in range(FFN_NC):
        gate, val = up_chunk(c)
        acts.append((gate * _sigmoid(gate) * val).astype(BF16))
    return h + _dot(jnp.concatenate(acts, axis=1), wd_ref[...])


def _prompt_ffn_body(h_ref, g2_ref, w_up_ref, fcw_ref, fcb_ref, wd_ref, gf_ref,
                     y_ref, state_ref, carry_scr, *, tm):
    t = pl.program_id(1)

    @pl.when(t == 0)
    def _zero_history():
        carry_scr[...] = jnp.zeros(carry_scr.shape, F32)

    h = h_ref[0]
    hn = _rmsnorm(h, g2_ref[...]).astype(BF16)

    def prev_rows(cols, up):
        ext = jnp.concatenate([carry_scr[:, cols], up], axis=0)
        carry_scr[:, cols] = up[tm - SUBLANES:, :]
        state_ref[0, :, cols] = up[tm - (FFN_CONV_W - 1):, :]
        return ext[SUBLANES - 1:SUBLANES - 1 + tm, :], ext[SUBLANES - 2:SUBLANES - 2 + tm, :]

    y = _ffn_chunks(h, hn, prev_rows, w_up_ref, fcw_ref, fcb_ref, wd_ref)
    y_ref[0] = _rmsnorm(y, gf_ref[...])


def _ffn_weight_specs():
    return [_resident((1, D)), _resident((D, 2 * D_FF)), _resident((FFN_CONV_W, 2 * D_FF)),
            _resident((1, 2 * D_FF)), _resident((D_FF, D)), _resident((1, D))]


def _prompt_ffn(h, ffn_w, *, tm):
    b, s, _ = h.shape
    row_block = pl.BlockSpec((1, tm, D), lambda i, t: (i, t, 0))
    state_shape = (b, FFN_CONV_W - 1, 2 * D_FF)
    return pl.pallas_call(
        functools.partial(_prompt_ffn_body, tm=tm),
        grid=(b, s // tm),
        in_specs=[row_block] + _ffn_weight_specs(),
        out_specs=[row_block, pl.BlockSpec((1,) + state_shape[1:], lambda i, t: (i, 0, 0))],
        out_shape=[jax.ShapeDtypeStruct((b, s, D), F32), jax.ShapeDtypeStruct(state_shape, F32)],
        scratch_shapes=[pltpu.VMEM((SUBLANES, 2 * D_FF), F32)],
        compiler_params=pltpu.CompilerParams(dimension_semantics=("arbitrary", "arbitrary"),
                                             vmem_limit_bytes=VMEM_LIMIT),
        name="prompt_ffn",
    )(h, *ffn_w)


def _sample_proj_body(x_ref, g1_ref, w_in_ref, q_ref, kvt_ref, u_ref, gates_ref):
    xn = _rmsnorm(x_ref[...], g1_ref[...]).astype(BF16)
    q_ref[...] = _dot(xn, w_in_ref[:, 0:D]) * SCALE
    kvt_ref[...] = lax.dot_general(w_in_ref[:, D:QKV_W], xn, (((0,), (1,)), ((), ())),
                                   preferred_element_type=F32)
    glu = _dot(xn, w_in_ref[:, GLU_OFF:GATE_OFF])
    u_ref[...] = glu[:, 0:D] * _sigmoid(glu[:, D:2 * D])
    gates_ref[...] = _sigmoid(_dot(xn, w_in_ref[:, GATE_OFF:IN_DIM]))


def _sample_proj(x2d, g1, w_in):
    m = x2d.shape[0]
    full = lambda r, c: pl.BlockSpec((r, c), lambda i: (0, 0))
    return pl.pallas_call(
        _sample_proj_body,
        grid=(1,),
        in_specs=[full(m, D), _resident((1, D)), _resident((D, IN_DIM))],
        out_specs=[full(m, D), full(2 * KV_DIM, m), full(m, D), full(m, 2 * D)],
        out_shape=[jax.ShapeDtypeStruct((m, D), F32), jax.ShapeDtypeStruct((2 * KV_DIM, m), F32),
                   jax.ShapeDtypeStruct((m, D), F32), jax.ShapeDtypeStruct((m, 2 * D), F32)],
        compiler_params=pltpu.CompilerParams(dimension_semantics=("arbitrary",),
                                             vmem_limit_bytes=VMEM_LIMIT),
        name="sample_proj",
    )(x2d, g1, w_in)


def _sample_mix_body(q_ref, kvt_ref, u_ref, kt_ref, vt_ref, cstate_ref, sink_ref,
                     cw_ref, cb_ref, lng_ref, lnb_ref,
                     attn_ref, cs_ref, kt_out_ref, vt_out_ref, cstate_out_ref,
                     qg_scr, *, gs, n_tok):
    rows_per_group = GROUP * n_tok
    keep = WINDOW - n_tok

    lane = lax.broadcasted_iota(jnp.int32, (N_KV, HEAD_DIM, WINDOW), 2)
    for n in range(gs):
        shift = (keep - n_tok * n) % WINDOW
        new_k = pltpu.roll(kvt_ref[0:KV_DIM, :], shift, axis=1).reshape(N_KV, HEAD_DIM, WINDOW)
        new_v = pltpu.roll(kvt_ref[KV_DIM:2 * KV_DIM, :], shift, axis=1).reshape(N_KV, HEAD_DIM, WINDOW)
        kt_out_ref[n] = jnp.where(lane >= keep, new_k, pltpu.roll(kt_ref[n], keep, axis=2))
        vt_out_ref[n] = jnp.where(lane >= keep, new_v, pltpu.roll(vt_ref[n], keep, axis=2))

    tok = lax.broadcasted_iota(jnp.int32, (rows_per_group, WINDOW), 0) % n_tok
    key = lax.broadcasted_iota(jnp.int32, (rows_per_group, WINDOW), 1)
    mask_old = (key > tok)[None]
    mask_new = jnp.logical_and(key >= keep, key - keep <= tok)[None]
    for g in range(N_KV):
        for hh in range(GROUP):
            h0 = (GROUP * g + hh) * HEAD_DIM
            qg_scr[:, hh * n_tok:(hh + 1) * n_tok, :] = q_ref[:, h0:h0 + HEAD_DIM].reshape(gs, n_tok, HEAD_DIM)
        qg = qg_scr[...].astype(BF16)
        s_old = jnp.einsum("nqd,ndk->nqk", qg, kt_ref[:, g].astype(BF16), preferred_element_type=F32)
        s_new = jnp.einsum("nqd,ndk->nqk", qg, kt_out_ref[:, g].astype(BF16), preferred_element_type=F32)
        s_old = jnp.where(mask_old, s_old, NEG_INF)
        s_new = jnp.where(mask_new, s_new, NEG_INF)
        sink = sink_ref[g][None]
        m = jnp.maximum(jnp.maximum(jnp.max(s_old, axis=-1, keepdims=True),
                                    jnp.max(s_new, axis=-1, keepdims=True)), sink)
        e_old = jnp.exp(s_old - m)
        e_new = jnp.exp(s_new - m)
        den = (jnp.sum(e_old, axis=-1, keepdims=True) + jnp.sum(e_new, axis=-1, keepdims=True)
               + jnp.exp(sink - m))
        o = (jnp.einsum("nqk,ndk->nqd", (e_old / den).astype(BF16), vt_ref[:, g].astype(BF16),
                        preferred_element_type=F32)
             + jnp.einsum("nqk,ndk->nqd", (e_new / den).astype(BF16), vt_out_ref[:, g].astype(BF16),
                          preferred_element_type=F32))
        for hh in range(GROUP):
            h0 = (GROUP * g + hh) * HEAD_DIM
            attn_ref[:, h0:h0 + HEAD_DIM] = o[:, hh * n_tok:(hh + 1) * n_tok, :].reshape(gs * n_tok, HEAD_DIM)

    n_state = CONV_W - 1
    u_slabs = [u_ref[:, i, :] for i in range(n_tok)]

    def ext(r, sl):
        return cstate_ref[r, :, sl] if r < n_state else u_slabs[r - n_state][:, sl]

    for r in range(n_state):
        cstate_out_ref[r] = ext(r + n_tok, slice(None))

    lane_chunk = 256
    for i in range(n_tok):
        pieces = []
        for lc in range(D // lane_chunk):
            sl = slice(lc * lane_chunk, (lc + 1) * lane_chunk)
            acc = jnp.broadcast_to(cb_ref[:, sl], (gs, lane_chunk))
            for j in range(CONV_W):
                w = jnp.concatenate([cw_ref[j, :, sl]] * (gs // SUBLANES), axis=0)
                acc = acc + ext(i + j, sl) * w
            pieces.append(acc)
        y = _layernorm(jnp.concatenate(pieces, axis=1), lng_ref[...], lnb_ref[...])
        cs_ref[:, i, :] = y * _sigmoid(y)


def _sample_mix(q, kvt, u, kt, vt, cstate, sink_rows, cw, cb, lng, lnb, *, gs):
    n_seq, n_tok, _ = u.shape
    rows = gs * n_tok
    assert rows % LANES == 0, "a group's token columns must fill whole lane tiles of the transposed k/v"
    row_block = pl.BlockSpec((rows, D), lambda s: (s, 0))
    seq_block = pl.BlockSpec((gs, n_tok, D), lambda s: (s, 0, 0))
    cache_block = pl.BlockSpec((gs, N_KV, HEAD_DIM, WINDOW), lambda s: (s, 0, 0, 0))
    state_block = pl.BlockSpec((CONV_W - 1, gs, D), lambda s: (0, s, 0))
    rows_per_group = GROUP * n_tok
    return pl.pallas_call(
        functools.partial(_sample_mix_body, gs=gs, n_tok=n_tok),
        grid=(n_seq // gs,),
        in_specs=[row_block, pl.BlockSpec((2 * KV_DIM, rows), lambda s: (0, s)), seq_block,
                  cache_block, cache_block, state_block,
                  _resident((N_KV, rows_per_group, 1)),
                  _resident((CONV_W, SUBLANES, D)), _resident((1, D)), _resident((1, D)), _resident((1, D))],
        out_specs=[row_block, seq_block, cache_block, cache_block, state_block],
        out_shape=[jax.ShapeDtypeStruct((n_seq * n_tok, D), F32), jax.ShapeDtypeStruct((n_seq, n_tok, D), F32),
                   jax.ShapeDtypeStruct((n_seq, N_KV, HEAD_DIM, WINDOW), F32),
                   jax.ShapeDtypeStruct((n_seq, N_KV, HEAD_DIM, WINDOW), F32),
                   jax.ShapeDtypeStruct((CONV_W - 1, n_seq, D), F32)],
        scratch_shapes=[pltpu.VMEM((gs, rows_per_group, HEAD_DIM), F32)],
        compiler_params=pltpu.CompilerParams(dimension_semantics=("arbitrary",),
                                             vmem_limit_bytes=VMEM_LIMIT),
        name="sample_mix",
    )(q, kvt, u, kt, vt, cstate, sink_rows, cw, cb, lng, lnb)


def _sample_merge_body(x_ref, attn_ref, cs_ref, gates_ref, w_o_ref, w_pb_ref, w_out_ref, h_ref):
    x = x_ref[...]
    branch_a = _dot(attn_ref[...].astype(BF16), w_o_ref[...])
    branch_b = _dot(cs_ref[...].astype(BF16), w_pb_ref[...])
    merged = gates_ref[:, 0:D] * branch_a + gates_ref[:, D:2 * D] * branch_b
    h_ref[...] = x + _dot(merged.astype(BF16), w_out_ref[...])


def _sample_merge(x, attn, cs, gates, w_o, w_pb, w_out):
    m = x.shape[0]
    full = lambda c: pl.BlockSpec((m, c), lambda i: (0, 0))
    return pl.pallas_call(
        _sample_merge_body,
        grid=(1,),
        in_specs=[full(D), full(D), full(D), full(2 * D),
                  _resident((D, D)), _resident((D, D)), _resident((D, D))],
        out_specs=full(D),
        out_shape=jax.ShapeDtypeStruct((m, D), F32),
        compiler_params=pltpu.CompilerParams(dimension_semantics=("arbitrary",),
                                             vmem_limit_bytes=VMEM_LIMIT),
        name="sample_merge",
    )(x, attn, cs, gates, w_o, w_pb, w_out)


def _sample_ffn_body(h_ref, state_ref, g2_ref, w_up_ref, fcw_ref, fcb_ref, wd_ref, gf_ref,
                     y_ref, state_out_ref, *, n_seq, n_tok):
    m = n_seq * n_tok
    h = jnp.concatenate([h_ref[:, i, :] for i in range(n_tok)], axis=0)
    hn = _rmsnorm(h, g2_ref[...]).astype(BF16)

    def prev_rows(cols, up):
        s0 = state_ref[0, :, cols]
        s1 = state_ref[1, :, cols]
        state_out_ref[0, :, cols] = up[m - 2 * n_seq:m - n_seq, :]
        state_out_ref[1, :, cols] = up[m - n_seq:, :]
        prev1 = jnp.concatenate([s1, up[:m - n_seq, :]], axis=0)
        prev2 = jnp.concatenate([s0, s1, up[:m - 2 * n_seq, :]], axis=0)
        return prev1, prev2

    y = _ffn_chunks(h, hn, prev_rows, w_up_ref, fcw_ref, fcb_ref, wd_ref)
    y = _rmsnorm(y, gf_ref[...])
    for i in range(n_tok):
        y_ref[:, i, :] = y[i * n_seq:(i + 1) * n_seq, :]


def _sample_ffn(h, state, ffn_w, *, n_seq, n_tok):
    m = n_seq * n_tok
    state_shape = (FFN_CONV_W - 1, n_seq, 2 * D_FF)
    state_spec = pl.BlockSpec(state_shape, lambda i: (0, 0, 0))
    seq_major = pl.BlockSpec((n_seq, n_tok, D), lambda i: (0, 0, 0))
    return pl.pallas_call(
        functools.partial(_sample_ffn_body, n_seq=n_seq, n_tok=n_tok),
        grid=(1,),
        in_specs=[seq_major, state_spec] + _ffn_weight_specs(),
        out_specs=[seq_major, state_spec],
        out_shape=[jax.ShapeDtypeStruct((n_seq, n_tok, D), F32), jax.ShapeDtypeStruct(state_shape, F32)],
        compiler_params=pltpu.CompilerParams(dimension_semantics=("arbitrary",),
                                             vmem_limit_bytes=VMEM_LIMIT),
        name="sample_ffn",
    )(h, state, *ffn_w)


def _layer(x_prompt, x_sample, k_past, v_past, conv_b_past, conv_f_past,
           norm1_g, w_in, sinks, w_o_attn, conv_b_w, conv_b_bias, ln_b_g, ln_b_b, w_pb, w_out,
           norm2_g, w_up, ffn_conv_w, ffn_conv_b, w_down, norm_f_g, *, tm_a, tm_ffn, gs):
    n_seq, n_tok, _ = x_sample.shape
    row = lambda v: v.reshape(1, -1)
    g1, g2, gf = row(norm1_g), row(norm2_g), row(norm_f_g)
    cb, lng, lnb = row(conv_b_bias), row(ln_b_g), row(ln_b_b)
    w_in_b, w_o_b, w_pb_b, w_out_b = (w.astype(BF16) for w in (w_in, w_o_attn, w_pb, w_out))
    ffn_w = (g2, w_up.astype(BF16), ffn_conv_w, row(ffn_conv_b), w_down.astype(BF16), gf)

    cw8 = jnp.broadcast_to(conv_b_w[:, None, :], (CONV_W, SUBLANES, D))
    h_p, k_win_p, v_win_p, conv_b_p = _prompt_stage_a(
        x_prompt, g1, w_in_b, sinks, w_o_b, cw8, cb, lng, lnb, w_pb_b, w_out_b, tm=tm_a)
    y_p, conv_f_p = _prompt_ffn(h_p, ffn_w, tm=tm_ffn)

    m = n_seq * n_tok
    q, kvt, u, gates = _sample_proj(x_sample.reshape(m, D), g1, w_in_b)
    sink_rows = jnp.repeat(sinks.reshape(N_KV, GROUP), n_tok, axis=1)[:, :, None]
    attn, cs, kt_s, vt_s, cstate_s = _sample_mix(
        q, kvt, u.reshape(n_seq, n_tok, D),
        k_past.transpose(0, 2, 3, 1), v_past.transpose(0, 2, 3, 1), conv_b_past.transpose(1, 0, 2),
        sink_rows, cw8, cb, lng, lnb, gs=gs)
    conv_b_s = cstate_s.transpose(1, 0, 2)
    h_s = _sample_merge(x_sample.reshape(m, D), attn, cs.reshape(m, D), gates, w_o_b, w_pb_b, w_out_b)
    y_s, ffn_state_s = _sample_ffn(h_s.reshape(n_seq, n_tok, D), conv_f_past.transpose(1, 0, 2), ffn_w,
                                   n_seq=n_seq, n_tok=n_tok)
    conv_f_s = ffn_state_s.transpose(1, 0, 2)

    kv_shape = lambda a: a.reshape(a.shape[0], WINDOW, N_KV, HEAD_DIM)
    return (y_p, y_s,
            kv_shape(k_win_p), kv_shape(v_win_p), conv_b_p, conv_f_p,
            kt_s.transpose(0, 3, 1, 2), vt_s.transpose(0, 3, 1, 2), conv_b_s, conv_f_s)


def kernel(x_prompt, x_sample, cache_k_win, cache_v_win, state_conv_b, state_conv_ffn, norm1_g, w_in, sinks,
           w_o_attn, conv_b_w, conv_b_bias, ln_b_g, ln_b_b, w_pb, w_out, norm2_g, w_up, ffn_conv_w,
           ffn_conv_b, w_down, norm_f_g):
    assert w_in.shape[0] == 1, "single-layer stack only"
    res = _layer(x_prompt, x_sample, cache_k_win[0], cache_v_win[0], state_conv_b[0], state_conv_ffn[0],
                 norm1_g[0], w_in[0], sinks[0], w_o_attn[0], conv_b_w[0], conv_b_bias[0], ln_b_g[0],
                 ln_b_b[0], w_pb[0], w_out[0], norm2_g[0], w_up[0], ffn_conv_w[0], ffn_conv_b[0],
                 w_down[0], norm_f_g, tm_a=256, tm_ffn=512, gs=16)
    return res[:2] + tuple(r[None] for r in res[2:])
```

```python
import functools

import jax
import jax.numpy as jnp
from jax import lax
from jax.experimental import pallas as pl
from jax.experimental.pallas import tpu as pltpu

F32 = jnp.float32
BF16 = jnp.bfloat16

D = 1024
N_HEADS, N_KV, HEAD_DIM = 16, 4, 64
WINDOW = 128
KV_DIM = N_KV * HEAD_DIM
CONV_W = 31
D_FF = 2816
FFN_CONV_W = 3
EPS = 1e-6
NEG_INF = -1e30
SCALE = HEAD_DIM ** -0.5
QKV_W = D + 2 * KV_DIM
GLU_OFF = QKV_W
GATE_OFF = QKV_W + 2 * D
IN_DIM = QKV_W + 4 * D
LANES = 128
SUBLANES = 8
GROUP = N_HEADS // N_KV
ATTN_BATCH = 4
CONV_TILES = 8
FFN_CH = 256
FFN_NC = D_FF // FFN_CH
VMEM_LIMIT = 56 * 1024 * 1024


def _resident(shape):
    nd = len(shape)
    return pl.BlockSpec(shape, lambda *_: (0,) * nd, pipeline_mode=pl.Buffered(1))


def _sigmoid(x):
    return 1.0 / (1.0 + jnp.exp(-x))


def _rmsnorm(x, g):
    return x * lax.rsqrt(jnp.mean(x * x, axis=-1, keepdims=True) + EPS) * g


def _layernorm(x, g, b):
    mu = jnp.mean(x, axis=-1, keepdims=True)
    xc = x - mu
    var = jnp.mean(xc * xc, axis=-1, keepdims=True)
    return xc * lax.rsqrt(var + EPS) * g + b


def _dot(a, b):
    return jnp.dot(a, b, preferred_element_type=F32)


def _softmax_with_sink(s, mask, sink):
    s = jnp.where(mask, s, NEG_INF)
    m = jnp.maximum(jnp.max(s, axis=-1, keepdims=True), sink)
    e = jnp.exp(s - m)
    den = jnp.sum(e, axis=-1, keepdims=True) + jnp.exp(sink - m)
    return e / den


def _split_heads_block_diag(x, lo):
    xr = pltpu.roll(x, HEAD_DIM, axis=1)
    z = jnp.zeros_like(x)
    even = (jnp.where(lo, x, z), jnp.where(lo, z, xr))
    odd = (jnp.where(lo, xr, z), jnp.where(lo, z, x))
    return even, odd


def _prompt_a_body(x_ref, g1_ref, w_in_ref, sinks_ref, w_o_ref, cw_ref, cb_ref, lng_ref, lnb_ref,
                   w_pb_ref, w_out_ref,
                   h_ref, kwin_ref, vwin_ref, cbs_ref,
                   kbd_scr, vbd_scr, q_scr, attn_scr, xn_scr, ext_scr, carry_scr, c_scr, *, tm):
    t = pl.program_id(1)
    last = pl.num_programs(1) - 1

    @pl.when(t == 0)
    def _zero_history():
        kbd_scr[:, :, 0:WINDOW, :] = jnp.zeros((N_KV, 2, WINDOW, LANES), BF16)
        vbd_scr[:, :, 0:WINDOW, :] = jnp.zeros((N_KV, 2, WINDOW, LANES), BF16)
        carry_scr[...] = jnp.zeros(carry_scr.shape, F32)

    x = x_ref[0]
    xn_f32 = _rmsnorm(x, g1_ref[...])
    xn = xn_f32.astype(BF16)

    qkv = _dot(xn, w_in_ref[:, 0:QKV_W])
    q_scr[...] = qkv[:, 0:D].astype(BF16)
    k_new = qkv[:, D:D + KV_DIM]
    v_new = qkv[:, D + KV_DIM:QKV_W]

    @pl.when(t == last)
    def _emit_kv_window():
        kwin_ref[0] = k_new[tm - WINDOW:, :]
        vwin_ref[0] = v_new[tm - WINDOW:, :]

    lo = lax.broadcasted_iota(jnp.int32, (tm, LANES), 1) < HEAD_DIM
    for m in range(N_KV // 2):
        k_even, k_odd = _split_heads_block_diag(k_new[:, LANES * m:LANES * (m + 1)] * SCALE, lo)
        v_even, v_odd = _split_heads_block_diag(v_new[:, LANES * m:LANES * (m + 1)], lo)
        for half in range(2):
            kbd_scr[2 * m, half, WINDOW:, :] = k_even[half].astype(BF16)
            kbd_scr[2 * m + 1, half, WINDOW:, :] = k_odd[half].astype(BF16)
            vbd_scr[2 * m, half, WINDOW:, :] = v_even[half].astype(BF16)
            vbd_scr[2 * m + 1, half, WINDOW:, :] = v_odd[half].astype(BF16)

    row = lax.broadcasted_iota(jnp.int32, (WINDOW, 2 * WINDOW), 0)
    col = lax.broadcasted_iota(jnp.int32, (WINDOW, 2 * WINDOW), 1)
    band = jnp.logical_and(col > row, col <= row + WINDOW)
    first_mask = jnp.logical_and(band, col >= jnp.where(t > 0, 0, WINDOW))

    def scores(qb, g):
        r0 = qb * WINDOW
        c0 = 2 * LANES * g
        q_pairs = jnp.concatenate([q_scr[r0:r0 + WINDOW, c0:c0 + LANES],
                                   q_scr[r0:r0 + WINDOW, c0 + LANES:c0 + 2 * LANES]], axis=0)
        kbd = jnp.concatenate([kbd_scr[g, 0, r0:r0 + 2 * WINDOW, :],
                               kbd_scr[g, 1, r0:r0 + 2 * WINDOW, :]], axis=0)
        return lax.dot_general(q_pairs, kbd, (((1,), (1,)), ((), ())), preferred_element_type=F32)

    def attend(item, s):
        qb, g = item
        r0 = qb * WINDOW
        c0 = 2 * LANES * g
        mask = first_mask if qb == 0 else band
        probs = []
        for pair in range(2):
            halves = []
            for hh in range(2):
                sink = sinks_ref[GROUP * g + 2 * pair + hh]
                p = _softmax_with_sink(s[pair * WINDOW:(pair + 1) * WINDOW, 2 * WINDOW * hh:2 * WINDOW * (hh + 1)],
                                       mask, sink)
                halves.append(p.astype(BF16))
            probs.append(jnp.concatenate(halves, axis=1))
        vbd = jnp.concatenate([vbd_scr[g, 0, r0:r0 + 2 * WINDOW, :],
                               vbd_scr[g, 1, r0:r0 + 2 * WINDOW, :]], axis=0)
        o = _dot(jnp.concatenate(probs, axis=0), vbd)
        attn_scr[r0:r0 + WINDOW, c0:c0 + LANES] = o[0:WINDOW, :].astype(BF16)
        attn_scr[r0:r0 + WINDOW, c0 + LANES:c0 + 2 * LANES] = o[WINDOW:2 * WINDOW, :].astype(BF16)

    items = [(qb, g) for qb in range(tm // WINDOW) for g in range(N_KV)]
    batches = [items[i:i + ATTN_BATCH] for i in range(0, len(items), ATTN_BATCH)]
    s_next = [scores(*it) for it in batches[0]]
    for k, batch in enumerate(batches):
        s_cur = s_next
        if k + 1 < len(batches):
            s_next = [scores(*it) for it in batches[k + 1]]
        for it, s in zip(batch, s_cur):
            attend(it, s)

    kbd_scr[:, :, 0:WINDOW, :] = kbd_scr[:, :, tm:tm + WINDOW, :]
    vbd_scr[:, :, 0:WINDOW, :] = vbd_scr[:, :, tm:tm + WINDOW, :]
    branch_a = _dot(attn_scr[...], w_o_ref[...])

    n_tiles = tm // SUBLANES
    assert n_tiles >= CONV_W - 1, "a tap must not reach further back than the previous-sublane tiles"
    xn_scr[...] = xn_f32.reshape(SUBLANES, n_tiles, D)
    xn_perm = jnp.concatenate([xn_scr[:, p, :] for p in range(n_tiles)], axis=0).astype(BF16)
    glu = _dot(xn_perm, w_in_ref[:, GLU_OFF:GATE_OFF])
    u_perm = (glu[:, 0:D] * _sigmoid(glu[:, D:2 * D])).reshape(n_tiles, SUBLANES, D)
    ext_scr[n_tiles:2 * n_tiles] = u_perm
    prev_last = jnp.broadcast_to(carry_scr[...][:, None, :], (n_tiles, SUBLANES, D))
    first_sub = lax.broadcasted_iota(jnp.int32, (n_tiles, SUBLANES, D), 1) == 0
    ext_scr[0:n_tiles] = jnp.where(first_sub, prev_last, pltpu.roll(u_perm, 1, axis=1))

    @pl.when(t == last)
    def _emit_conv_state():
        cbs_ref[0] = ext_scr[2 * n_tiles - (CONV_W - 1):2 * n_tiles, SUBLANES - 1, :]

    carry_scr[...] = ext_scr[n_tiles:2 * n_tiles, SUBLANES - 1, :]

    tiles = CONV_TILES
    lane_chunk = 128

    def _conv_rows(i, carry):
        p0 = i * tiles
        for lc in range(D // lane_chunk):
            sl = slice(lc * lane_chunk, (lc + 1) * lane_chunk)
            acc = jnp.broadcast_to(cb_ref[:, sl], (tiles * SUBLANES, lane_chunk))
            for j in range(CONV_W):
                back = CONV_W - 1 - j
                w = jnp.concatenate([cw_ref[j, :, sl]] * tiles, axis=0)
                win = ext_scr[pl.ds(n_tiles + p0 - back, tiles), :, sl]
                acc = acc + win.reshape(tiles * SUBLANES, lane_chunk) * w
            c_scr[pl.ds(p0, tiles), :, sl] = acc.reshape(tiles, SUBLANES, lane_chunk)
        return carry

    lax.fori_loop(0, n_tiles // tiles, _conv_rows, 0)

    y = _layernorm(c_scr[...].reshape(tm, D), lng_ref[...], lnb_ref[...])
    c_scr[...] = _dot((y * _sigmoid(y)).astype(BF16), w_pb_ref[...]).reshape(n_tiles, SUBLANES, D)
    branch_b = jnp.concatenate(
        [c_scr[(n * SUBLANES) % n_tiles:(n * SUBLANES) % n_tiles + SUBLANES, (n * SUBLANES) // n_tiles, :]
         for n in range(n_tiles)], axis=0)

    gates = _dot(xn, w_in_ref[:, GATE_OFF:IN_DIM])
    merged = _sigmoid(gates[:, 0:D]) * branch_a + _sigmoid(gates[:, D:2 * D]) * branch_b
    h_ref[0] = x + _dot(merged.astype(BF16), w_out_ref[...])


def _prompt_stage_a(x, g1, w_in, sinks, w_o, cw, cb, lng, lnb, w_pb, w_out, *, tm):
    b, s, _ = x.shape
    grid = (b, s // tm)
    row_block = pl.BlockSpec((1, tm, D), lambda i, t: (i, t, 0))
    per_seq = lambda r, c: pl.BlockSpec((1, r, c), lambda i, t: (i, 0, 0))
    return pl.pallas_call(
        functools.partial(_prompt_a_body, tm=tm),
        grid=grid,
        in_specs=[row_block, _resident((1, D)), _resident((D, IN_DIM)),
                  pl.BlockSpec(memory_space=pltpu.SMEM),
                  _resident((D, D)), _resident((CONV_W, SUBLANES, D)), _resident((1, D)), _resident((1, D)),
                  _resident((1, D)), _resident((D, D)), _resident((D, D))],
        out_specs=[row_block, per_seq(WINDOW, KV_DIM), per_seq(WINDOW, KV_DIM), per_seq(CONV_W - 1, D)],
        out_shape=[jax.ShapeDtypeStruct((b, s, D), F32),
                   jax.ShapeDtypeStruct((b, WINDOW, KV_DIM), F32),
                   jax.ShapeDtypeStruct((b, WINDOW, KV_DIM), F32),
                   jax.ShapeDtypeStruct((b, CONV_W - 1, D), F32)],
        scratch_shapes=[pltpu.VMEM((N_KV, 2, WINDOW + tm, LANES), BF16),
                        pltpu.VMEM((N_KV, 2, WINDOW + tm, LANES), BF16),
                        pltpu.VMEM((tm, D), BF16),
                        pltpu.VMEM((tm, D), BF16),
                        pltpu.VMEM((SUBLANES, tm // SUBLANES, D), F32),
                        pltpu.VMEM((2 * tm // SUBLANES, SUBLANES, D), F32),
                        pltpu.VMEM((tm // SUBLANES, D), F32),
                        pltpu.VMEM((tm // SUBLANES, SUBLANES, D), F32)],
        compiler_params=pltpu.CompilerParams(dimension_semantics=("arbitrary", "arbitrary"),
                                             vmem_limit_bytes=VMEM_LIMIT),
        name="prompt_stage_a",
    )(x, g1, w_in, sinks, w_o, cw, cb, lng, lnb, w_pb, w_out)


def _ffn_chunks(h, hn, prev_rows, w_up_ref, fcw_ref, fcb_ref, wd_ref):
    def up_chunk(c):
        halves = []
        for part in range(2):
            cols = slice(part * D_FF + c * FFN_CH, part * D_FF + (c + 1) * FFN_CH)
            up = _dot(hn, w_up_ref[:, cols])
            prev1, prev2 = prev_rows(cols, up)
            w = fcw_ref[:, cols]
            halves.append(w[0:1, :] * prev2 + w[1:2, :] * prev1 + w[2:3, :] * up + fcb_ref[:, cols])
        return halves

    acts = []
    for c in range(FFN_NC):
        gate, val = up_chunk(c)
        acts.append((gate * _sigmoid(gate) * val).astype(BF16))
    return h + _dot(jnp.concatenate(acts, axis=1), wd_ref[...])


def _prompt_ffn_body(h_ref, g2_ref, w_up_ref, fcw_ref, fcb_ref, wd_ref, gf_ref,
                     y_ref, state_ref, carry_scr, *, tm):
    t = pl.program_id(1)

    @pl.when(t == 0)
    def _zero_history():
        carry_scr[...] = jnp.zeros(carry_scr.shape, F32)

    h = h_ref[0]
    hn = _rmsnorm(h, g2_ref[...]).astype(BF16)

    def prev_rows(cols, up):
        n_t = tm // SUBLANES
        ext = jnp.concatenate([carry_scr[:, cols], up], axis=0).reshape(n_t + 1, SUBLANES, FFN_CH)
        carry_scr[:, cols] = up[tm - SUBLANES:, :]
        state_ref[0, :, cols] = up[tm - (FFN_CONV_W - 1):, :]
        sub = lax.broadcasted_iota(jnp.int32, (n_t, SUBLANES, FFN_CH), 1)
        shifted = []
        for k in (1, 2):
            r = pltpu.roll(ext, k, axis=1)
            shifted.append(jnp.where(sub < k, r[0:n_t], r[1:n_t + 1]).reshape(tm, FFN_CH))
        return shifted[0], shifted[1]

    y = _ffn_chunks(h, hn, prev_rows, w_up_ref, fcw_ref, fcb_ref, wd_ref)
    y_ref[0] = _rmsnorm(y, gf_ref[...])


def _ffn_weight_specs():
    return [_resident((1, D)), _resident((D, 2 * D_FF)), _resident((FFN_CONV_W, 2 * D_FF)),
            _resident((1, 2 * D_FF)), _resident((D_FF, D)), _resident((1, D))]


def _prompt_ffn(h, ffn_w, *, tm):
    b, s, _ = h.shape
    row_block = pl.BlockSpec((1, tm, D), lambda i, t: (i, t, 0))
    state_shape = (b, FFN_CONV_W - 1, 2 * D_FF)
    return pl.pallas_call(
        functools.partial(_prompt_ffn_body, tm=tm),
        grid=(b, s // tm),
        in_specs=[row_block] + _ffn_weight_specs(),
        out_specs=[row_block, pl.BlockSpec((1,) + state_shape[1:], lambda i, t: (i, 0, 0))],
        out_shape=[jax.ShapeDtypeStruct((b, s, D), F32), jax.ShapeDtypeStruct(state_shape, F32)],
        scratch_shapes=[pltpu.VMEM((SUBLANES, 2 * D_FF), F32)],
        compiler_params=pltpu.CompilerParams(dimension_semantics=("arbitrary", "arbitrary"),
                                             vmem_limit_bytes=VMEM_LIMIT),
        name="prompt_ffn",
    )(h, *ffn_w)


def _sample_proj_body(x_ref, g1_ref, w_in_ref, q_ref, kvt_ref, u_ref, gates_ref):
    xn = _rmsnorm(x_ref[...], g1_ref[...]).astype(BF16)
    q_ref[...] = _dot(xn, w_in_ref[:, 0:D]) * SCALE
    kvt_ref[...] = lax.dot_general(w_in_ref[:, D:QKV_W], xn, (((0,), (1,)), ((), ())),
                                   preferred_element_type=F32)
    glu = _dot(xn, w_in_ref[:, GLU_OFF:GATE_OFF])
    u_ref[...] = glu[:, 0:D] * _sigmoid(glu[:, D:2 * D])
    gates_ref[...] = _sigmoid(_dot(xn, w_in_ref[:, GATE_OFF:IN_DIM]))


def _sample_proj(x2d, g1, w_in):
    m = x2d.shape[0]
    full = lambda r, c: pl.BlockSpec((r, c), lambda i: (0, 0))
    return pl.pallas_call(
        _sample_proj_body,
        grid=(1,),
        in_specs=[full(m, D), _resident((1, D)), _resident((D, IN_DIM))],
        out_specs=[full(m, D), full(2 * KV_DIM, m), full(m, D), full(m, 2 * D)],
        out_shape=[jax.ShapeDtypeStruct((m, D), F32), jax.ShapeDtypeStruct((2 * KV_DIM, m), F32),
                   jax.ShapeDtypeStruct((m, D), F32), jax.ShapeDtypeStruct((m, 2 * D), F32)],
        compiler_params=pltpu.CompilerParams(dimension_semantics=("arbitrary",),
                                             vmem_limit_bytes=VMEM_LIMIT),
        name="sample_proj",
    )(x2d, g1, w_in)


def _sample_mix_body(q_ref, kvt_ref, u_ref, kt_ref, vt_ref, cstate_ref, sink_ref,
                     cw_ref, cb_ref, lng_ref, lnb_ref,
                     attn_ref, cs_ref, kt_out_ref, vt_out_ref, cstate_out_ref,
                     qg_scr, *, gs, n_tok):
    rows_per_group = GROUP * n_tok
    keep = WINDOW - n_tok

    lane = lax.broadcasted_iota(jnp.int32, (N_KV, HEAD_DIM, WINDOW), 2)
    for n in range(gs):
        shift = (keep - n_tok * n) % WINDOW
        new_k = pltpu.roll(kvt_ref[0:KV_DIM, :], shift, axis=1).reshape(N_KV, HEAD_DIM, WINDOW)
        new_v = pltpu.roll(kvt_ref[KV_DIM:2 * KV_DIM, :], shift, axis=1).reshape(N_KV, HEAD_DIM, WINDOW)
        kt_out_ref[n] = jnp.where(lane >= keep, new_k, pltpu.roll(kt_ref[n], keep, axis=2))
        vt_out_ref[n] = jnp.where(lane >= keep, new_v, pltpu.roll(vt_ref[n], keep, axis=2))

    tok = lax.broadcasted_iota(jnp.int32, (rows_per_group, WINDOW), 0) % n_tok
    key = lax.broadcasted_iota(jnp.int32, (rows_per_group, WINDOW), 1)
    mask_old = (key > tok)[None]
    mask_new = jnp.logical_and(key >= keep, key - keep <= tok)[None]
    for g in range(N_KV):
        for hh in range(GROUP):
            h0 = (GROUP * g + hh) * HEAD_DIM
            qg_scr[:, hh * n_tok:(hh + 1) * n_tok, :] = q_ref[:, h0:h0 + HEAD_DIM].reshape(gs, n_tok, HEAD_DIM)
        qg = qg_scr[...].astype(BF16)
        s_old = jnp.einsum("nqd,ndk->nqk", qg, kt_ref[:, g].astype(BF16), preferred_element_type=F32)
        s_new = jnp.einsum("nqd,ndk->nqk", qg, kt_out_ref[:, g].astype(BF16), preferred_element_type=F32)
        s_old = jnp.where(mask_old, s_old, NEG_INF)
        s_new = jnp.where(mask_new, s_new, NEG_INF)
        sink = sink_ref[g][None]
        m = jnp.maximum(jnp.maximum(jnp.max(s_old, axis=-1, keepdims=True),
                                    jnp.max(s_new, axis=-1, keepdims=True)), sink)
        e_old = jnp.exp(s_old - m)
        e_new = jnp.exp(s_new - m)
        den = (jnp.sum(e_old, axis=-1, keepdims=True) + jnp.sum(e_new, axis=-1, keepdims=True)
               + jnp.exp(sink - m))
        o = (jnp.einsum("nqk,ndk->nqd", (e_old / den).astype(BF16), vt_ref[:, g].astype(BF16),
                        preferred_element_type=F32)
             + jnp.einsum("nqk,ndk->nqd", (e_new / den).astype(BF16), vt_out_ref[:, g].astype(BF16),
                          preferred_element_type=F32))
        for hh in range(GROUP):
            h0 = (GROUP * g + hh) * HEAD_DIM
            attn_ref[:, h0:h0 + HEAD_DIM] = o[:, hh * n_tok:(hh + 1) * n_tok, :].reshape(gs * n_tok, HEAD_DIM)

    n_state = CONV_W - 1
    u_slabs = [u_ref[:, i, :] for i in range(n_tok)]

    def ext(r, sl):
        return cstate_ref[r, :, sl] if r < n_state else u_slabs[r - n_state][:, sl]

    for r in range(n_state):
        cstate_out_ref[r] = ext(r + n_tok, slice(None))

    lane_chunk = 256
    for i in range(n_tok):
        pieces = []
        for lc in range(D // lane_chunk):
            sl = slice(lc * lane_chunk, (lc + 1) * lane_chunk)
            acc = jnp.broadcast_to(cb_ref[:, sl], (gs, lane_chunk))
            for j in range(CONV_W):
                w = jnp.concatenate([cw_ref[j, :, sl]] * (gs // SUBLANES), axis=0)
                acc = acc + ext(i + j, sl) * w
            pieces.append(acc)
        y = _layernorm(jnp.concatenate(pieces, axis=1), lng_ref[...], lnb_ref[...])
        cs_ref[:, i, :] = y * _sigmoid(y)


def _sample_mix(q, kvt, u, kt, vt, cstate, sink_rows, cw, cb, lng, lnb, *, gs):
    n_seq, n_tok, _ = u.shape
    rows = gs * n_tok
    assert rows % LANES == 0, "a group's token columns must fill whole lane tiles of the transposed k/v"
    row_block = pl.BlockSpec((rows, D), lambda s: (s, 0))
    seq_block = pl.BlockSpec((gs, n_tok, D), lambda s: (s, 0, 0))
    cache_block = pl.BlockSpec((gs, N_KV, HEAD_DIM, WINDOW), lambda s: (s, 0, 0, 0))
    state_block = pl.BlockSpec((CONV_W - 1, gs, D), lambda s: (0, s, 0))
    rows_per_group = GROUP * n_tok
    return pl.pallas_call(
        functools.partial(_sample_mix_body, gs=gs, n_tok=n_tok),
        grid=(n_seq // gs,),
        in_specs=[row_block, pl.BlockSpec((2 * KV_DIM, rows), lambda s: (0, s)), seq_block,
                  cache_block, cache_block, state_block,
                  _resident((N_KV, rows_per_group, 1)),
                  _resident((CONV_W, SUBLANES, D)), _resident((1, D)), _resident((1, D)), _resident((1, D))],
        out_specs=[row_block, seq_block, cache_block, cache_block, state_block],
        out_shape=[jax.ShapeDtypeStruct((n_seq * n_tok, D), F32), jax.ShapeDtypeStruct((n_seq, n_tok, D), F32),
                   jax.ShapeDtypeStruct((n_seq, N_KV, HEAD_DIM, WINDOW), F32),
                   jax.ShapeDtypeStruct((n_seq, N_KV, HEAD_DIM, WINDOW), F32),
                   jax.ShapeDtypeStruct((CONV_W - 1, n_seq, D), F32)],
        scratch_shapes=[pltpu.VMEM((gs, rows_per_group, HEAD_DIM), F32)],
        compiler_params=pltpu.CompilerParams(dimension_semantics=("arbitrary",),
                                             vmem_limit_bytes=VMEM_LIMIT),
        name="sample_mix",
    )(q, kvt, u, kt, vt, cstate, sink_rows, cw, cb, lng, lnb)


def _sample_merge_body(x_ref, attn_ref, cs_ref, gates_ref, w_o_ref, w_pb_ref, w_out_ref, h_ref):
    x = x_ref[...]
    branch_a = _dot(attn_ref[...].astype(BF16), w_o_ref[...])
    branch_b = _dot(cs_ref[...].astype(BF16), w_pb_ref[...])
    merged = gates_ref[:, 0:D] * branch_a + gates_ref[:, D:2 * D] * branch_b
    h_ref[...] = x + _dot(merged.astype(BF16), w_out_ref[...])


def _sample_merge(x, attn, cs, gates, w_o, w_pb, w_out):
    m = x.shape[0]
    full = lambda c: pl.BlockSpec((m, c), lambda i: (0, 0))
    return pl.pallas_call(
        _sample_merge_body,
        grid=(1,),
        in_specs=[full(D), full(D), full(D), full(2 * D),
                  _resident((D, D)), _resident((D, D)), _resident((D, D))],
        out_specs=full(D),
        out_shape=jax.ShapeDtypeStruct((m, D), F32),
        compiler_params=pltpu.CompilerParams(dimension_semantics=("arbitrary",),
                                             vmem_limit_bytes=VMEM_LIMIT),
        name="sample_merge",
    )(x, attn, cs, gates, w_o, w_pb, w_out)


def _sample_ffn_body(h_ref, state_ref, g2_ref, w_up_ref, fcw_ref, fcb_ref, wd_ref, gf_ref,
                     y_ref, state_out_ref, *, n_seq, n_tok):
    m = n_seq * n_tok
    h = jnp.concatenate([h_ref[:, i, :] for i in range(n_tok)], axis=0)
    hn = _rmsnorm(h, g2_ref[...]).astype(BF16)

    def prev_rows(cols, up):
        s0 = state_ref[0, :, cols]
        s1 = state_ref[1, :, cols]
        state_out_ref[0, :, cols] = up[m - 2 * n_seq:m - n_seq, :]
        state_out_ref[1, :, cols] = up[m - n_seq:, :]
        prev1 = jnp.concatenate([s1, up[:m - n_seq, :]], axis=0)
        prev2 = jnp.concatenate([s0, s1, up[:m - 2 * n_seq, :]], axis=0)
        return prev1, prev2

    y = _ffn_chunks(h, hn, prev_rows, w_up_ref, fcw_ref, fcb_ref, wd_ref)
    y = _rmsnorm(y, gf_ref[...])
    for i in range(n_tok):
        y_ref[:, i, :] = y[i * n_seq:(i + 1) * n_seq, :]


def _sample_ffn(h, state, ffn_w, *, n_seq, n_tok):
    m = n_seq * n_tok
    state_shape = (FFN_CONV_W - 1, n_seq, 2 * D_FF)
    state_spec = pl.BlockSpec(state_shape, lambda i: (0, 0, 0))
    seq_major = pl.BlockSpec((n_seq, n_tok, D), lambda i: (0, 0, 0))
    return pl.pallas_call(
        functools.partial(_sample_ffn_body, n_seq=n_seq, n_tok=n_tok),
        grid=(1,),
        in_specs=[seq_major, state_spec] + _ffn_weight_specs(),
        out_specs=[seq_major, state_spec],
        out_shape=[jax.ShapeDtypeStruct((n_seq, n_tok, D), F32), jax.ShapeDtypeStruct(state_shape, F32)],
        compiler_params=pltpu.CompilerParams(dimension_semantics=("arbitrary",),
                                             vmem_limit_bytes=VMEM_LIMIT),
        name="sample_ffn",
    )(h, state, *ffn_w)


def _layer(x_prompt, x_sample, k_past, v_past, conv_b_past, conv_f_past,
           norm1_g, w_in, sinks, w_o_attn, conv_b_w, conv_b_bias, ln_b_g, ln_b_b, w_pb, w_out,
           norm2_g, w_up, ffn_conv_w, ffn_conv_b, w_down, norm_f_g, *, tm_a, tm_ffn, gs):
    n_seq, n_tok, _ = x_sample.shape
    row = lambda v: v.reshape(1, -1)
    g1, g2, gf = row(norm1_g), row(norm2_g), row(norm_f_g)
    cb, lng, lnb = row(conv_b_bias), row(ln_b_g), row(ln_b_b)
    w_in_b, w_o_b, w_pb_b, w_out_b = (w.astype(BF16) for w in (w_in, w_o_attn, w_pb, w_out))
    ffn_w = (g2, w_up.astype(BF16), ffn_conv_w, row(ffn_conv_b), w_down.astype(BF16), gf)

    cw8 = jnp.broadcast_to(conv_b_w[:, None, :], (CONV_W, SUBLANES, D))
    h_p, k_win_p, v_win_p, conv_b_p = _prompt_stage_a(
        x_prompt, g1, w_in_b, sinks, w_o_b, cw8, cb, lng, lnb, w_pb_b, w_out_b, tm=tm_a)
    y_p, conv_f_p = _prompt_ffn(h_p, ffn_w, tm=tm_ffn)

    m = n_seq * n_tok
    q, kvt, u, gates = _sample_proj(x_sample.reshape(m, D), g1, w_in_b)
    sink_rows = jnp.repeat(sinks.reshape(N_KV, GROUP), n_tok, axis=1)[:, :, None]
    attn, cs, kt_s, vt_s, cstate_s = _sample_mix(
        q, kvt, u.reshape(n_seq, n_tok, D),
        k_past.transpose(0, 2, 3, 1), v_past.transpose(0, 2, 3, 1), conv_b_past.transpose(1, 0, 2),
        sink_rows, cw8, cb, lng, lnb, gs=gs)
    conv_b_s = cstate_s.transpose(1, 0, 2)
    h_s = _sample_merge(x_sample.reshape(m, D), attn, cs.reshape(m, D), gates, w_o_b, w_pb_b, w_out_b)
    y_s, ffn_state_s = _sample_ffn(h_s.reshape(n_seq, n_tok, D), conv_f_past.transpose(1, 0, 2), ffn_w,
                                   n_seq=n_seq, n_tok=n_tok)
    conv_f_s = ffn_state_s.transpose(1, 0, 2)

    kv_shape = lambda a: a.reshape(a.shape[0], WINDOW, N_KV, HEAD_DIM)
    return (y_p, y_s,
            kv_shape(k_win_p), kv_shape(v_win_p), conv_b_p, conv_f_p,
            kt_s.transpose(0, 3, 1, 2), vt_s.transpose(0, 3, 1, 2), conv_b_s, conv_f_s)


def kernel(x_prompt, x_sample, cache_k_win, cache_v_win, state_conv_b, state_conv_ffn, norm1_g, w_in, sinks,
           w_o_attn, conv_b_w, conv_b_bias, ln_b_g, ln_b_b, w_pb, w_out, norm2_g, w_up, ffn_conv_w,
           ffn_conv_b, w_down, norm_f_g):
    assert w_in.shape[0] == 1, "single-layer stack only"
    res = _layer(x_prompt, x_sample, cache_k_win[0], cache_v_win[0], state_conv_b[0], state_conv_ffn[0],
                 norm1_g[0], w_in[0], sinks[0], w_o_attn[0], conv_b_w[0], conv_b_bias[0], ln_b_g[0],
                 ln_b_b[0], w_pb[0], w_out[0], norm2_g[0], w_up[0], ffn_conv_w[0], ffn_conv_b[0],
                 w_down[0], norm_f_g, tm_a=512, tm_ffn=512, gs=16)
    return res[:2] + tuple(r[None] for r in res[2:])
```

```python
import functools

import jax
import jax.numpy as jnp
from jax import lax
from jax.experimental import pallas as pl
from jax.experimental.pallas import tpu as pltpu

F32 = jnp.float32
BF16 = jnp.bfloat16

D = 1024
N_HEADS, N_KV, HEAD_DIM = 16, 4, 64
WINDOW = 128
KV_DIM = N_KV * HEAD_DIM
CONV_W = 31
D_FF = 2816
FFN_CONV_W = 3
EPS = 1e-6
NEG_INF = -1e30
SCALE = HEAD_DIM ** -0.5
QKV_W = D + 2 * KV_DIM
GLU_OFF = QKV_W
GATE_OFF = QKV_W + 2 * D
IN_DIM = QKV_W + 4 * D
LANES = 128
SUBLANES = 8
GROUP = N_HEADS // N_KV
ATTN_BATCH = 4
PROMPT_TILE = 512
SAMPLE_GROUP = 16
COL_PIECE = 256
SAMPLE_ROW_STEPS = 2
CONV_TILES = 8
FFN_CH = 256
FFN_NC = D_FF // FFN_CH
VMEM_LIMIT = 56 * 1024 * 1024


def _resident(shape):
    nd = len(shape)
    return pl.BlockSpec(shape, lambda *_: (0,) * nd, pipeline_mode=pl.Buffered(1))


def _sigmoid(x):
    return 1.0 / (1.0 + jnp.exp(-x))


def _rmsnorm(x, g):
    return x * lax.rsqrt(jnp.mean(x * x, axis=-1, keepdims=True) + EPS) * g


def _layernorm(x, g, b):
    mu = jnp.mean(x, axis=-1, keepdims=True)
    xc = x - mu
    var = jnp.mean(xc * xc, axis=-1, keepdims=True)
    return xc * lax.rsqrt(var + EPS) * g + b


def _dot(a, b):
    return jnp.dot(a, b, preferred_element_type=F32)


def _softmax_with_sink(s, mask, sink):
    s = jnp.where(mask, s, NEG_INF)
    m = jnp.maximum(jnp.max(s, axis=-1, keepdims=True), sink)
    e = jnp.exp(s - m)
    den = jnp.sum(e, axis=-1, keepdims=True) + jnp.exp(sink - m)
    return e / den


def _split_heads_block_diag(x, lo):
    xr = pltpu.roll(x, HEAD_DIM, axis=1)
    z = jnp.zeros_like(x)
    even = (jnp.where(lo, x, z), jnp.where(lo, z, xr))
    odd = (jnp.where(lo, xr, z), jnp.where(lo, z, x))
    return even, odd


def _prompt_a_body(x_ref, g1_ref, w_in_ref, sinks_ref, w_o_ref, cw_ref, cb_ref, lng_ref, lnb_ref,
                   w_pb_ref, w_out_ref,
                   h_ref, kwin_ref, vwin_ref, cbs_ref,
                   kbd_scr, vbd_scr, q_scr, attn_scr, xn_scr, ext_scr, carry_scr, c_scr, *, tm):
    t = pl.program_id(1)
    last = pl.num_programs(1) - 1

    @pl.when(t == 0)
    def _zero_history():
        kbd_scr[:, :, 0:WINDOW, :] = jnp.zeros((N_KV, 2, WINDOW, LANES), BF16)
        vbd_scr[:, :, 0:WINDOW, :] = jnp.zeros((N_KV, 2, WINDOW, LANES), BF16)
        carry_scr[...] = jnp.zeros(carry_scr.shape, F32)

    x = x_ref[0]
    xn_f32 = _rmsnorm(x, g1_ref[...])
    xn = xn_f32.astype(BF16)

    qkv = _dot(xn, w_in_ref[:, 0:QKV_W])
    q_scr[...] = qkv[:, 0:D].astype(BF16)
    k_new = qkv[:, D:D + KV_DIM]
    v_new = qkv[:, D + KV_DIM:QKV_W]

    @pl.when(t == last)
    def _emit_kv_window():
        kwin_ref[0] = k_new[tm - WINDOW:, :]
        vwin_ref[0] = v_new[tm - WINDOW:, :]

    lo = lax.broadcasted_iota(jnp.int32, (tm, LANES), 1) < HEAD_DIM
    for m in range(N_KV // 2):
        k_even, k_odd = _split_heads_block_diag(k_new[:, LANES * m:LANES * (m + 1)] * SCALE, lo)
        v_even, v_odd = _split_heads_block_diag(v_new[:, LANES * m:LANES * (m + 1)], lo)
        for half in range(2):
            kbd_scr[2 * m, half, WINDOW:, :] = k_even[half].astype(BF16)
            kbd_scr[2 * m + 1, half, WINDOW:, :] = k_odd[half].astype(BF16)
            vbd_scr[2 * m, half, WINDOW:, :] = v_even[half].astype(BF16)
            vbd_scr[2 * m + 1, half, WINDOW:, :] = v_odd[half].astype(BF16)

    row = lax.broadcasted_iota(jnp.int32, (WINDOW, 2 * WINDOW), 0)
    col = lax.broadcasted_iota(jnp.int32, (WINDOW, 2 * WINDOW), 1)
    band = jnp.logical_and(col > row, col <= row + WINDOW)
    first_mask = jnp.logical_and(band, col >= jnp.where(t > 0, 0, WINDOW))

    def scores(qb, g):
        r0 = qb * WINDOW
        c0 = 2 * LANES * g
        q_pairs = jnp.concatenate([q_scr[r0:r0 + WINDOW, c0:c0 + LANES],
                                   q_scr[r0:r0 + WINDOW, c0 + LANES:c0 + 2 * LANES]], axis=0)
        kbd = jnp.concatenate([kbd_scr[g, 0, r0:r0 + 2 * WINDOW, :],
                               kbd_scr[g, 1, r0:r0 + 2 * WINDOW, :]], axis=0)
        return lax.dot_general(q_pairs, kbd, (((1,), (1,)), ((), ())), preferred_element_type=F32)

    def attend(item, s):
        qb, g = item
        r0 = qb * WINDOW
        c0 = 2 * LANES * g
        mask = first_mask if qb == 0 else band
        probs = []
        for pair in range(2):
            halves = []
            for hh in range(2):
                sink = sinks_ref[GROUP * g + 2 * pair + hh]
                p = _softmax_with_sink(s[pair * WINDOW:(pair + 1) * WINDOW, 2 * WINDOW * hh:2 * WINDOW * (hh + 1)],
                                       mask, sink)
                halves.append(p.astype(BF16))
            probs.append(jnp.concatenate(halves, axis=1))
        vbd = jnp.concatenate([vbd_scr[g, 0, r0:r0 + 2 * WINDOW, :],
                               vbd_scr[g, 1, r0:r0 + 2 * WINDOW, :]], axis=0)
        o = _dot(jnp.concatenate(probs, axis=0), vbd)
        attn_scr[r0:r0 + WINDOW, c0:c0 + LANES] = o[0:WINDOW, :].astype(BF16)
        attn_scr[r0:r0 + WINDOW, c0 + LANES:c0 + 2 * LANES] = o[WINDOW:2 * WINDOW, :].astype(BF16)

    items = [(qb, g) for qb in range(tm // WINDOW) for g in range(N_KV)]
    batches = [items[i:i + ATTN_BATCH] for i in range(0, len(items), ATTN_BATCH)]
    s_next = [scores(*it) for it in batches[0]]
    for k, batch in enumerate(batches):
        s_cur = s_next
        if k + 1 < len(batches):
            s_next = [scores(*it) for it in batches[k + 1]]
        for it, s in zip(batch, s_cur):
            attend(it, s)

    kbd_scr[:, :, 0:WINDOW, :] = kbd_scr[:, :, tm:tm + WINDOW, :]
    vbd_scr[:, :, 0:WINDOW, :] = vbd_scr[:, :, tm:tm + WINDOW, :]
    branch_a = _dot(attn_scr[...], w_o_ref[...])

    n_tiles = tm // SUBLANES
    assert n_tiles >= CONV_W - 1, "a tap must not reach further back than the previous-sublane tiles"
    xn_scr[...] = xn_f32.reshape(SUBLANES, n_tiles, D)
    xn_perm = jnp.concatenate([xn_scr[:, p, :] for p in range(n_tiles)], axis=0).astype(BF16)
    u_pieces = []
    for c0 in range(0, D, COL_PIECE):
        a = _dot(xn_perm, w_in_ref[:, GLU_OFF + c0:GLU_OFF + c0 + COL_PIECE])
        b = _dot(xn_perm, w_in_ref[:, GLU_OFF + D + c0:GLU_OFF + D + c0 + COL_PIECE])
        u_pieces.append(a * _sigmoid(b))
    u_perm = jnp.concatenate(u_pieces, axis=1).reshape(n_tiles, SUBLANES, D)
    ext_scr[n_tiles:2 * n_tiles] = u_perm
    prev_last = jnp.broadcast_to(carry_scr[...][:, None, :], (n_tiles, SUBLANES, D))
    first_sub = lax.broadcasted_iota(jnp.int32, (n_tiles, SUBLANES, D), 1) == 0
    ext_scr[0:n_tiles] = jnp.where(first_sub, prev_last, pltpu.roll(u_perm, 1, axis=1))

    @pl.when(t == last)
    def _emit_conv_state():
        cbs_ref[0] = ext_scr[2 * n_tiles - (CONV_W - 1):2 * n_tiles, SUBLANES - 1, :]

    carry_scr[...] = ext_scr[n_tiles:2 * n_tiles, SUBLANES - 1, :]

    tiles = CONV_TILES
    lane_chunk = LANES

    def _conv_rows(i, carry):
        p0 = i * tiles
        for lc in range(D // lane_chunk):
            sl = slice(lc * lane_chunk, (lc + 1) * lane_chunk)
            acc = jnp.broadcast_to(cb_ref[:, sl], (tiles * SUBLANES, lane_chunk))
            for j in range(CONV_W):
                back = CONV_W - 1 - j
                w = jnp.concatenate([cw_ref[j, :, sl]] * tiles, axis=0)
                win = ext_scr[pl.ds(n_tiles + p0 - back, tiles), :, sl]
                acc = acc + win.reshape(tiles * SUBLANES, lane_chunk) * w
            c_scr[pl.ds(p0, tiles), :, sl] = acc.reshape(tiles, SUBLANES, lane_chunk)
        return carry

    lax.fori_loop(0, n_tiles // tiles, _conv_rows, 0)

    y = _layernorm(c_scr[...].reshape(tm, D), lng_ref[...], lnb_ref[...])
    c_scr[...] = _dot((y * _sigmoid(y)).astype(BF16), w_pb_ref[...]).reshape(n_tiles, SUBLANES, D)
    branch_b = jnp.concatenate(
        [c_scr[(n * SUBLANES) % n_tiles:(n * SUBLANES) % n_tiles + SUBLANES, (n * SUBLANES) // n_tiles, :]
         for n in range(n_tiles)], axis=0)

    merged = []
    for c0 in range(0, D, COL_PIECE):
        ga = _dot(xn, w_in_ref[:, GATE_OFF + c0:GATE_OFF + c0 + COL_PIECE])
        gb = _dot(xn, w_in_ref[:, GATE_OFF + D + c0:GATE_OFF + D + c0 + COL_PIECE])
        merged.append((_sigmoid(ga) * branch_a[:, c0:c0 + COL_PIECE]
                       + _sigmoid(gb) * branch_b[:, c0:c0 + COL_PIECE]).astype(BF16))
    h_ref[0] = x + _dot(jnp.concatenate(merged, axis=1), w_out_ref[...])


def _prompt_stage_a(x, g1, w_in, sinks, w_o, cw, cb, lng, lnb, w_pb, w_out, *, tm):
    b, s, _ = x.shape
    grid = (b, s // tm)
    row_block = pl.BlockSpec((1, tm, D), lambda i, t: (i, t, 0))
    per_seq = lambda r, c: pl.BlockSpec((1, r, c), lambda i, t: (i, 0, 0))
    return pl.pallas_call(
        functools.partial(_prompt_a_body, tm=tm),
        grid=grid,
        in_specs=[row_block, _resident((1, D)), _resident((D, IN_DIM)),
                  pl.BlockSpec(memory_space=pltpu.SMEM),
                  _resident((D, D)), _resident((CONV_W, SUBLANES, D)), _resident((1, D)), _resident((1, D)),
                  _resident((1, D)), _resident((D, D)), _resident((D, D))],
        out_specs=[row_block, per_seq(WINDOW, KV_DIM), per_seq(WINDOW, KV_DIM), per_seq(CONV_W - 1, D)],
        out_shape=[jax.ShapeDtypeStruct((b, s, D), F32),
                   jax.ShapeDtypeStruct((b, WINDOW, KV_DIM), F32),
                   jax.ShapeDtypeStruct((b, WINDOW, KV_DIM), F32),
                   jax.ShapeDtypeStruct((b, CONV_W - 1, D), F32)],
        scratch_shapes=[pltpu.VMEM((N_KV, 2, WINDOW + tm, LANES), BF16),
                        pltpu.VMEM((N_KV, 2, WINDOW + tm, LANES), BF16),
                        pltpu.VMEM((tm, D), BF16),
                        pltpu.VMEM((tm, D), BF16),
                        pltpu.VMEM((SUBLANES, tm // SUBLANES, D), F32),
                        pltpu.VMEM((2 * tm // SUBLANES, SUBLANES, D), F32),
                        pltpu.VMEM((tm // SUBLANES, D), F32),
                        pltpu.VMEM((tm // SUBLANES, SUBLANES, D), F32)],
        compiler_params=pltpu.CompilerParams(dimension_semantics=("arbitrary", "arbitrary"),
                                             vmem_limit_bytes=VMEM_LIMIT),
        name="prompt_stage_a",
    )(x, g1, w_in, sinks, w_o, cw, cb, lng, lnb, w_pb, w_out)


def _ffn_chunks(h, hn, prev_rows, w_up_ref, fcw_ref, fcb_ref, wd_ref):
    def up_chunk(c):
        halves = []
        for part in range(2):
            cols = slice(part * D_FF + c * FFN_CH, part * D_FF + (c + 1) * FFN_CH)
            up = _dot(hn, w_up_ref[:, cols])
            prev1, prev2 = prev_rows(cols, up)
            w = fcw_ref[:, cols]
            halves.append(w[0:1, :] * prev2 + w[1:2, :] * prev1 + w[2:3, :] * up + fcb_ref[:, cols])
        return halves

    acts = []
    for c in range(FFN_NC):
        gate, val = up_chunk(c)
        acts.append((gate * _sigmoid(gate) * val).astype(BF16))
    return h + _dot(jnp.concatenate(acts, axis=1), wd_ref[...])


def _prompt_ffn_body(h_ref, g2_ref, w_up_ref, fcw_ref, fcb_ref, wd_ref, gf_ref,
                     y_ref, state_ref, carry_scr, *, tm):
    t = pl.program_id(1)

    @pl.when(t == 0)
    def _zero_history():
        carry_scr[...] = jnp.zeros(carry_scr.shape, F32)

    h = h_ref[0]
    hn = _rmsnorm(h, g2_ref[...]).astype(BF16)

    def prev_rows(cols, up):
        n_t = tm // SUBLANES
        ext = jnp.concatenate([carry_scr[:, cols], up], axis=0).reshape(n_t + 1, SUBLANES, FFN_CH)
        carry_scr[:, cols] = up[tm - SUBLANES:, :]
        state_ref[0, :, cols] = up[tm - (FFN_CONV_W - 1):, :]
        sub = lax.broadcasted_iota(jnp.int32, (n_t, SUBLANES, FFN_CH), 1)
        shifted = []
        for k in (1, 2):
            r = pltpu.roll(ext, k, axis=1)
            shifted.append(jnp.where(sub < k, r[0:n_t], r[1:n_t + 1]).reshape(tm, FFN_CH))
        return shifted[0], shifted[1]

    y = _ffn_chunks(h, hn, prev_rows, w_up_ref, fcw_ref, fcb_ref, wd_ref)
    y_ref[0] = _rmsnorm(y, gf_ref[...])


def _ffn_weight_specs():
    return [_resident((1, D)), _resident((D, 2 * D_FF)), _resident((FFN_CONV_W, 2 * D_FF)),
            _resident((1, 2 * D_FF)), _resident((D_FF, D)), _resident((1, D))]


def _prompt_ffn(h, ffn_w, *, tm):
    b, s, _ = h.shape
    row_block = pl.BlockSpec((1, tm, D), lambda i, t: (i, t, 0))
    state_shape = (b, FFN_CONV_W - 1, 2 * D_FF)
    return pl.pallas_call(
        functools.partial(_prompt_ffn_body, tm=tm),
        grid=(b, s // tm),
        in_specs=[row_block] + _ffn_weight_specs(),
        out_specs=[row_block, pl.BlockSpec((1,) + state_shape[1:], lambda i, t: (i, 0, 0))],
        out_shape=[jax.ShapeDtypeStruct((b, s, D), F32), jax.ShapeDtypeStruct(state_shape, F32)],
        scratch_shapes=[pltpu.VMEM((SUBLANES, 2 * D_FF), F32)],
        compiler_params=pltpu.CompilerParams(dimension_semantics=("arbitrary", "arbitrary"),
                                             vmem_limit_bytes=VMEM_LIMIT),
        name="prompt_ffn",
    )(h, *ffn_w)


def _sample_proj_body(x_ref, g1_ref, w_in_ref, q_ref, kvt_ref, u_ref, gates_ref):
    xn = _rmsnorm(x_ref[...], g1_ref[...]).astype(BF16)
    q_ref[...] = _dot(xn, w_in_ref[:, 0:D]) * SCALE
    kvt_ref[...] = lax.dot_general(w_in_ref[:, D:QKV_W], xn, (((0,), (1,)), ((), ())),
                                   preferred_element_type=F32)
    glu = _dot(xn, w_in_ref[:, GLU_OFF:GATE_OFF])
    u_ref[...] = glu[:, 0:D] * _sigmoid(glu[:, D:2 * D])
    gates_ref[...] = _sigmoid(_dot(xn, w_in_ref[:, GATE_OFF:IN_DIM]))


def _sample_proj(x2d, g1, w_in):
    m = x2d.shape[0]
    rows = m // SAMPLE_ROW_STEPS
    blk = lambda c: pl.BlockSpec((rows, c), lambda i: (i, 0))
    return pl.pallas_call(
        _sample_proj_body,
        grid=(SAMPLE_ROW_STEPS,),
        in_specs=[blk(D), _resident((1, D)), _resident((D, IN_DIM))],
        out_specs=[blk(D), pl.BlockSpec((2 * KV_DIM, rows), lambda i: (0, i)), blk(D), blk(2 * D)],
        out_shape=[jax.ShapeDtypeStruct((m, D), F32), jax.ShapeDtypeStruct((2 * KV_DIM, m), F32),
                   jax.ShapeDtypeStruct((m, D), F32), jax.ShapeDtypeStruct((m, 2 * D), F32)],
        compiler_params=pltpu.CompilerParams(dimension_semantics=("arbitrary",),
                                             vmem_limit_bytes=VMEM_LIMIT),
        name="sample_proj",
    )(x2d, g1, w_in)


def _sample_mix_body(q_ref, kvt_ref, u_ref, kt_ref, vt_ref, cstate_ref, sink_ref,
                     cw_ref, cb_ref, lng_ref, lnb_ref,
                     attn_ref, cs_ref, kt_out_ref, vt_out_ref, cstate_out_ref,
                     qg_scr, *, gs, n_tok):
    rows_per_group = GROUP * n_tok
    keep = WINDOW - n_tok

    lane = lax.broadcasted_iota(jnp.int32, (N_KV, HEAD_DIM, WINDOW), 2)
    for n in range(gs):
        shift = (keep - n_tok * n) % WINDOW
        new_k = pltpu.roll(kvt_ref[0:KV_DIM, :], shift, axis=1).reshape(N_KV, HEAD_DIM, WINDOW)
        new_v = pltpu.roll(kvt_ref[KV_DIM:2 * KV_DIM, :], shift, axis=1).reshape(N_KV, HEAD_DIM, WINDOW)
        kt_out_ref[n] = jnp.where(lane >= keep, new_k, pltpu.roll(kt_ref[n], keep, axis=2))
        vt_out_ref[n] = jnp.where(lane >= keep, new_v, pltpu.roll(vt_ref[n], keep, axis=2))

    tok = lax.broadcasted_iota(jnp.int32, (rows_per_group, WINDOW), 0) % n_tok
    key = lax.broadcasted_iota(jnp.int32, (rows_per_group, WINDOW), 1)
    mask_old = (key > tok)[None]
    mask_new = jnp.logical_and(key >= keep, key - keep <= tok)[None]
    for g in range(N_KV):
        for hh in range(GROUP):
            h0 = (GROUP * g + hh) * HEAD_DIM
            qg_scr[:, hh * n_tok:(hh + 1) * n_tok, :] = q_ref[:, h0:h0 + HEAD_DIM].reshape(gs, n_tok, HEAD_DIM)
        qg = qg_scr[...].astype(BF16)
        s_old = jnp.einsum("nqd,ndk->nqk", qg, kt_ref[:, g].astype(BF16), preferred_element_type=F32)
        s_new = jnp.einsum("nqd,ndk->nqk", qg, kt_out_ref[:, g].astype(BF16), preferred_element_type=F32)
        s_old = jnp.where(mask_old, s_old, NEG_INF)
        s_new = jnp.where(mask_new, s_new, NEG_INF)
        sink = sink_ref[g][None]
        m = jnp.maximum(jnp.maximum(jnp.max(s_old, axis=-1, keepdims=True),
                                    jnp.max(s_new, axis=-1, keepdims=True)), sink)
        e_old = jnp.exp(s_old - m)
        e_new = jnp.exp(s_new - m)
        den = (jnp.sum(e_old, axis=-1, keepdims=True) + jnp.sum(e_new, axis=-1, keepdims=True)
               + jnp.exp(sink - m))
        o = (jnp.einsum("nqk,ndk->nqd", (e_old / den).astype(BF16), vt_ref[:, g].astype(BF16),
                        preferred_element_type=F32)
             + jnp.einsum("nqk,ndk->nqd", (e_new / den).astype(BF16), vt_out_ref[:, g].astype(BF16),
                          preferred_element_type=F32))
        for hh in range(GROUP):
            h0 = (GROUP * g + hh) * HEAD_DIM
            attn_ref[:, h0:h0 + HEAD_DIM] = o[:, hh * n_tok:(hh + 1) * n_tok, :].reshape(gs * n_tok, HEAD_DIM)

    n_state = CONV_W - 1
    u_slabs = [u_ref[:, i, :] for i in range(n_tok)]

    def ext(r, sl):
        return cstate_ref[r, :, sl] if r < n_state else u_slabs[r - n_state][:, sl]

    for r in range(n_state):
        cstate_out_ref[r] = ext(r + n_tok, slice(None))

    lane_chunk = 2 * LANES
    for i in range(n_tok):
        pieces = []
        for lc in range(D // lane_chunk):
            sl = slice(lc * lane_chunk, (lc + 1) * lane_chunk)
            acc = jnp.broadcast_to(cb_ref[:, sl], (gs, lane_chunk))
            for j in range(CONV_W):
                w = jnp.concatenate([cw_ref[j, :, sl]] * (gs // SUBLANES), axis=0)
                acc = acc + ext(i + j, sl) * w
            pieces.append(acc)
        y = _layernorm(jnp.concatenate(pieces, axis=1), lng_ref[...], lnb_ref[...])
        cs_ref[:, i, :] = y * _sigmoid(y)


def _sample_mix(q, kvt, u, kt, vt, cstate, sink_rows, cw, cb, lng, lnb, *, gs):
    n_seq, n_tok, _ = u.shape
    rows = gs * n_tok
    assert rows % LANES == 0, "a group's token columns must fill whole lane tiles of the transposed k/v"
    row_block = pl.BlockSpec((rows, D), lambda s: (s, 0))
    seq_block = pl.BlockSpec((gs, n_tok, D), lambda s: (s, 0, 0))
    cache_block = pl.BlockSpec((gs, N_KV, HEAD_DIM, WINDOW), lambda s: (s, 0, 0, 0))
    state_block = pl.BlockSpec((CONV_W - 1, gs, D), lambda s: (0, s, 0))
    rows_per_group = GROUP * n_tok
    return pl.pallas_call(
        functools.partial(_sample_mix_body, gs=gs, n_tok=n_tok),
        grid=(n_seq // gs,),
        in_specs=[row_block, pl.BlockSpec((2 * KV_DIM, rows), lambda s: (0, s)), seq_block,
                  cache_block, cache_block, state_block,
                  _resident((N_KV, rows_per_group, 1)),
                  _resident((CONV_W, SUBLANES, D)), _resident((1, D)), _resident((1, D)), _resident((1, D))],
        out_specs=[row_block, seq_block, cache_block, cache_block, state_block],
        out_shape=[jax.ShapeDtypeStruct((n_seq * n_tok, D), F32), jax.ShapeDtypeStruct((n_seq, n_tok, D), F32),
                   jax.ShapeDtypeStruct((n_seq, N_KV, HEAD_DIM, WINDOW), F32),
                   jax.ShapeDtypeStruct((n_seq, N_KV, HEAD_DIM, WINDOW), F32),
                   jax.ShapeDtypeStruct((CONV_W - 1, n_seq, D), F32)],
        scratch_shapes=[pltpu.VMEM((gs, rows_per_group, HEAD_DIM), F32)],
        compiler_params=pltpu.CompilerParams(dimension_semantics=("arbitrary",),
                                             vmem_limit_bytes=VMEM_LIMIT),
        name="sample_mix",
    )(q, kvt, u, kt, vt, cstate, sink_rows, cw, cb, lng, lnb)


def _sample_merge_body(x_ref, attn_ref, cs_ref, gates_ref, w_o_ref, w_pb_ref, w_out_ref, h_ref):
    x = x_ref[...]
    branch_a = _dot(attn_ref[...].astype(BF16), w_o_ref[...])
    branch_b = _dot(cs_ref[...].astype(BF16), w_pb_ref[...])
    merged = gates_ref[:, 0:D] * branch_a + gates_ref[:, D:2 * D] * branch_b
    h_ref[...] = x + _dot(merged.astype(BF16), w_out_ref[...])


def _sample_merge(x, attn, cs, gates, w_o, w_pb, w_out):
    m = x.shape[0]
    blk = lambda c: pl.BlockSpec((m // SAMPLE_ROW_STEPS, c), lambda i: (i, 0))
    return pl.pallas_call(
        _sample_merge_body,
        grid=(SAMPLE_ROW_STEPS,),
        in_specs=[blk(D), blk(D), blk(D), blk(2 * D),
                  _resident((D, D)), _resident((D, D)), _resident((D, D))],
        out_specs=blk(D),
        out_shape=jax.ShapeDtypeStruct((m, D), F32),
        compiler_params=pltpu.CompilerParams(dimension_semantics=("arbitrary",),
                                             vmem_limit_bytes=VMEM_LIMIT),
        name="sample_merge",
    )(x, attn, cs, gates, w_o, w_pb, w_out)


def _sample_ffn_body(h_ref, state_ref, g2_ref, w_up_ref, fcw_ref, fcb_ref, wd_ref, gf_ref,
                     y_ref, state_out_ref, *, n_seq, n_tok):
    m = n_seq * n_tok
    h = jnp.concatenate([h_ref[:, i, :] for i in range(n_tok)], axis=0)
    hn = _rmsnorm(h, g2_ref[...]).astype(BF16)

    def prev_rows(cols, up):
        s0 = state_ref[0, :, cols]
        s1 = state_ref[1, :, cols]
        state_out_ref[0, :, cols] = up[m - 2 * n_seq:m - n_seq, :]
        state_out_ref[1, :, cols] = up[m - n_seq:, :]
        prev1 = jnp.concatenate([s1, up[:m - n_seq, :]], axis=0)
        prev2 = jnp.concatenate([s0, s1, up[:m - 2 * n_seq, :]], axis=0)
        return prev1, prev2

    y = _ffn_chunks(h, hn, prev_rows, w_up_ref, fcw_ref, fcb_ref, wd_ref)
    y = _rmsnorm(y, gf_ref[...])
    for i in range(n_tok):
        y_ref[:, i, :] = y[i * n_seq:(i + 1) * n_seq, :]


def _sample_ffn(h, state, ffn_w, *, n_seq, n_tok):
    m = n_seq * n_tok
    state_shape = (FFN_CONV_W - 1, n_seq, 2 * D_FF)
    state_spec = pl.BlockSpec(state_shape, lambda i: (0, 0, 0))
    seq_major = pl.BlockSpec((n_seq, n_tok, D), lambda i: (0, 0, 0))
    return pl.pallas_call(
        functools.partial(_sample_ffn_body, n_seq=n_seq, n_tok=n_tok),
        grid=(1,),
        in_specs=[seq_major, state_spec] + _ffn_weight_specs(),
        out_specs=[seq_major, state_spec],
        out_shape=[jax.ShapeDtypeStruct((n_seq, n_tok, D), F32), jax.ShapeDtypeStruct(state_shape, F32)],
        compiler_params=pltpu.CompilerParams(dimension_semantics=("arbitrary",),
                                             vmem_limit_bytes=VMEM_LIMIT),
        name="sample_ffn",
    )(h, state, *ffn_w)


def _layer(x_prompt, x_sample, k_past, v_past, conv_b_past, conv_f_past,
           norm1_g, w_in, sinks, w_o_attn, conv_b_w, conv_b_bias, ln_b_g, ln_b_b, w_pb, w_out,
           norm2_g, w_up, ffn_conv_w, ffn_conv_b, w_down, norm_f_g, *, tm_a, tm_ffn, gs):
    n_seq, n_tok, _ = x_sample.shape
    row = lambda v: v.reshape(1, -1)
    g1, g2, gf = row(norm1_g), row(norm2_g), row(norm_f_g)
    cb, lng, lnb = row(conv_b_bias), row(ln_b_g), row(ln_b_b)
    w_in_b, w_o_b, w_pb_b, w_out_b = (w.astype(BF16) for w in (w_in, w_o_attn, w_pb, w_out))
    ffn_w = (g2, w_up.astype(BF16), ffn_conv_w, row(ffn_conv_b), w_down.astype(BF16), gf)

    cw8 = jnp.broadcast_to(conv_b_w[:, None, :], (CONV_W, SUBLANES, D))
    h_p, k_win_p, v_win_p, conv_b_p = _prompt_stage_a(
        x_prompt, g1, w_in_b, sinks, w_o_b, cw8, cb, lng, lnb, w_pb_b, w_out_b, tm=tm_a)
    y_p, conv_f_p = _prompt_ffn(h_p, ffn_w, tm=tm_ffn)

    m = n_seq * n_tok
    q, kvt, u, gates = _sample_proj(x_sample.reshape(m, D), g1, w_in_b)
    sink_rows = jnp.repeat(sinks.reshape(N_KV, GROUP), n_tok, axis=1)[:, :, None]
    attn, cs, kt_s, vt_s, cstate_s = _sample_mix(
        q, kvt, u.reshape(n_seq, n_tok, D),
        k_past.transpose(0, 2, 3, 1), v_past.transpose(0, 2, 3, 1), conv_b_past.transpose(1, 0, 2),
        sink_rows, cw8, cb, lng, lnb, gs=gs)
    conv_b_s = cstate_s.transpose(1, 0, 2)
    h_s = _sample_merge(x_sample.reshape(m, D), attn, cs.reshape(m, D), gates, w_o_b, w_pb_b, w_out_b)
    y_s, ffn_state_s = _sample_ffn(h_s.reshape(n_seq, n_tok, D), conv_f_past.transpose(1, 0, 2), ffn_w,
                                   n_seq=n_seq, n_tok=n_tok)
    conv_f_s = ffn_state_s.transpose(1, 0, 2)

    kv_shape = lambda a: a.reshape(a.shape[0], WINDOW, N_KV, HEAD_DIM)
    return (y_p, y_s,
            kv_shape(k_win_p), kv_shape(v_win_p), conv_b_p, conv_f_p,
            kt_s.transpose(0, 3, 1, 2), vt_s.transpose(0, 3, 1, 2), conv_b_s, conv_f_s)


def kernel(x_prompt, x_sample, cache_k_win, cache_v_win, state_conv_b, state_conv_ffn, norm1_g, w_in, sinks,
           w_o_attn, conv_b_w, conv_b_bias, ln_b_g, ln_b_b, w_pb, w_out, norm2_g, w_up, ffn_conv_w,
           ffn_conv_b, w_down, norm_f_g):
    assert w_in.shape[0] == 1, "single-layer stack only"
    res = _layer(x_prompt, x_sample, cache_k_win[0], cache_v_win[0], state_conv_b[0], state_conv_ffn[0],
                 norm1_g[0], w_in[0], sinks[0], w_o_attn[0], conv_b_w[0], conv_b_bias[0], ln_b_g[0],
                 ln_b_b[0], w_pb[0], w_out[0], norm2_g[0], w_up[0], ffn_conv_w[0], ffn_conv_b[0],
                 w_down[0], norm_f_g, tm_a=PROMPT_TILE, tm_ffn=PROMPT_TILE, gs=SAMPLE_GROUP)
    return res[:2] + tuple(r[None] for r in res[2:])
```

```python
import functools

import jax
import jax.numpy as jnp
from jax import lax
from jax.experimental import pallas as pl
from jax.experimental.pallas import tpu as pltpu

F32 = jnp.float32
BF16 = jnp.bfloat16

D = 1024
N_HEADS, N_KV, HEAD_DIM = 16, 4, 64
WINDOW = 128
KV_DIM = N_KV * HEAD_DIM
CONV_W = 31
D_FF = 2816
FFN_CONV_W = 3
EPS = 1e-6
NEG_INF = -1e30
NEG_LOG2E = -1.4426950408889634
SCALE = HEAD_DIM ** -0.5
QKV_W = D + 2 * KV_DIM
GLU_OFF = QKV_W
GATE_OFF = QKV_W + 2 * D
IN_DIM = QKV_W + 4 * D
LANES = 128
SUBLANES = 8
GROUP = N_HEADS // N_KV
ATTN_BATCH = 8
PROMPT_TILE = 512
FFN_TILE = 1024
SAMPLE_GROUP = 16
COL_PIECE = 256
SAMPLE_ROW_STEPS = 2
CONV_TILES = 8
FFN_CH = 256
FFN_NC = D_FF // FFN_CH
VMEM_LIMIT = 56 * 1024 * 1024


def _resident(shape):
    nd = len(shape)
    return pl.BlockSpec(shape, lambda *_: (0,) * nd, pipeline_mode=pl.Buffered(1))


def _sigmoid(x):
    return 1.0 / (1.0 + jnp.exp2(x * NEG_LOG2E))


def _rmsnorm(x, g):
    return x * lax.rsqrt(jnp.mean(x * x, axis=-1, keepdims=True) + EPS) * g


def _layernorm(x, g, b):
    mu = jnp.mean(x, axis=-1, keepdims=True)
    xc = x - mu
    var = jnp.mean(xc * xc, axis=-1, keepdims=True)
    return xc * lax.rsqrt(var + EPS) * g + b


def _dot(a, b):
    return jnp.dot(a, b, preferred_element_type=F32)


def _softmax_with_sink(s, mask, sink):
    s = jnp.where(mask, s, NEG_INF)
    m = jnp.maximum(jnp.max(s, axis=-1, keepdims=True), sink)
    e = jnp.exp(s - m)
    den = jnp.sum(e, axis=-1, keepdims=True) + jnp.exp(sink - m)
    return e / den


def _split_heads_block_diag(x, lo):
    xr = pltpu.roll(x, HEAD_DIM, axis=1)
    z = jnp.zeros_like(x)
    even = (jnp.where(lo, x, z), jnp.where(lo, z, xr))
    odd = (jnp.where(lo, xr, z), jnp.where(lo, z, x))
    return even, odd


def _prompt_a_body(x_ref, g1_ref, w_in_ref, sinks_ref, w_o_ref, cw_ref, cb_ref, lng_ref, lnb_ref,
                   w_pb_ref, w_out_ref,
                   h_ref, kwin_ref, vwin_ref, cbs_ref,
                   kbd_scr, vbd_scr, q_scr, attn_scr, xn_scr, ext_scr, carry_scr, c_scr, *, tm):
    t = pl.program_id(1)
    last = pl.num_programs(1) - 1

    @pl.when(t == 0)
    def _zero_history():
        kbd_scr[:, :, 0:WINDOW, :] = jnp.zeros((N_KV, 2, WINDOW, LANES), BF16)
        vbd_scr[:, :, 0:WINDOW, :] = jnp.zeros((N_KV, 2, WINDOW, LANES), BF16)
        carry_scr[...] = jnp.zeros(carry_scr.shape, F32)

    x = x_ref[0]
    xn_f32 = _rmsnorm(x, g1_ref[...])
    xn = xn_f32.astype(BF16)

    qkv = _dot(xn, w_in_ref[:, 0:QKV_W])
    q_scr[...] = qkv[:, 0:D].astype(BF16)
    k_new = qkv[:, D:D + KV_DIM]
    v_new = qkv[:, D + KV_DIM:QKV_W]

    @pl.when(t == last)
    def _emit_kv_window():
        kwin_ref[0] = k_new[tm - WINDOW:, :]
        vwin_ref[0] = v_new[tm - WINDOW:, :]

    lo = lax.broadcasted_iota(jnp.int32, (tm, LANES), 1) < HEAD_DIM
    for m in range(N_KV // 2):
        k_even, k_odd = _split_heads_block_diag(k_new[:, LANES * m:LANES * (m + 1)] * SCALE, lo)
        v_even, v_odd = _split_heads_block_diag(v_new[:, LANES * m:LANES * (m + 1)], lo)
        for half in range(2):
            kbd_scr[2 * m, half, WINDOW:, :] = k_even[half].astype(BF16)
            kbd_scr[2 * m + 1, half, WINDOW:, :] = k_odd[half].astype(BF16)
            vbd_scr[2 * m, half, WINDOW:, :] = v_even[half].astype(BF16)
            vbd_scr[2 * m + 1, half, WINDOW:, :] = v_odd[half].astype(BF16)

    row = lax.broadcasted_iota(jnp.int32, (WINDOW, 2 * WINDOW), 0)
    col = lax.broadcasted_iota(jnp.int32, (WINDOW, 2 * WINDOW), 1)
    band = jnp.logical_and(col > row, col <= row + WINDOW)
    first_mask = jnp.logical_and(band, col >= jnp.where(t > 0, 0, WINDOW))

    def scores(qb, g):
        r0 = qb * WINDOW
        c0 = 2 * LANES * g
        q_pairs = jnp.concatenate([q_scr[r0:r0 + WINDOW, c0:c0 + LANES],
                                   q_scr[r0:r0 + WINDOW, c0 + LANES:c0 + 2 * LANES]], axis=0)
        kbd = jnp.concatenate([kbd_scr[g, 0, r0:r0 + 2 * WINDOW, :],
                               kbd_scr[g, 1, r0:r0 + 2 * WINDOW, :]], axis=0)
        return lax.dot_general(q_pairs, kbd, (((1,), (1,)), ((), ())), preferred_element_type=F32)

    def attend(item, s):
        qb, g = item
        r0 = qb * WINDOW
        c0 = 2 * LANES * g
        mask = first_mask if qb == 0 else band
        probs = []
        for pair in range(2):
            halves = []
            for hh in range(2):
                sink = sinks_ref[GROUP * g + 2 * pair + hh]
                p = _softmax_with_sink(s[pair * WINDOW:(pair + 1) * WINDOW, 2 * WINDOW * hh:2 * WINDOW * (hh + 1)],
                                       mask, sink)
                halves.append(p.astype(BF16))
            probs.append(jnp.concatenate(halves, axis=1))
        vbd = jnp.concatenate([vbd_scr[g, 0, r0:r0 + 2 * WINDOW, :],
                               vbd_scr[g, 1, r0:r0 + 2 * WINDOW, :]], axis=0)
        o = _dot(jnp.concatenate(probs, axis=0), vbd)
        attn_scr[r0:r0 + WINDOW, c0:c0 + LANES] = o[0:WINDOW, :].astype(BF16)
        attn_scr[r0:r0 + WINDOW, c0 + LANES:c0 + 2 * LANES] = o[WINDOW:2 * WINDOW, :].astype(BF16)

    items = [(qb, g) for qb in range(tm // WINDOW) for g in range(N_KV)]
    batches = [items[i:i + ATTN_BATCH] for i in range(0, len(items), ATTN_BATCH)]
    s_next = [scores(*it) for it in batches[0]]
    for k, batch in enumerate(batches):
        s_cur = s_next
        if k + 1 < len(batches):
            s_next = [scores(*it) for it in batches[k + 1]]
        for it, s in zip(batch, s_cur):
            attend(it, s)

    kbd_scr[:, :, 0:WINDOW, :] = kbd_scr[:, :, tm:tm + WINDOW, :]
    vbd_scr[:, :, 0:WINDOW, :] = vbd_scr[:, :, tm:tm + WINDOW, :]
    branch_a = _dot(attn_scr[...], w_o_ref[...])

    n_tiles = tm // SUBLANES
    assert n_tiles >= CONV_W - 1, "a tap must not reach further back than the previous-sublane tiles"
    xn_scr[...] = xn_f32.reshape(SUBLANES, n_tiles, D)
    xn_perm = jnp.concatenate([xn_scr[:, p, :] for p in range(n_tiles)], axis=0).astype(BF16)
    u_pieces = []
    for c0 in range(0, D, COL_PIECE):
        a = _dot(xn_perm, w_in_ref[:, GLU_OFF + c0:GLU_OFF + c0 + COL_PIECE])
        b = _dot(xn_perm, w_in_ref[:, GLU_OFF + D + c0:GLU_OFF + D + c0 + COL_PIECE])
        u_pieces.append(a * _sigmoid(b))
    u_perm = jnp.concatenate(u_pieces, axis=1).reshape(n_tiles, SUBLANES, D)
    ext_scr[n_tiles:2 * n_tiles] = u_perm
    prev_last = jnp.broadcast_to(carry_scr[...][:, None, :], (n_tiles, SUBLANES, D))
    first_sub = lax.broadcasted_iota(jnp.int32, (n_tiles, SUBLANES, D), 1) == 0
    ext_scr[0:n_tiles] = jnp.where(first_sub, prev_last, pltpu.roll(u_perm, 1, axis=1))

    @pl.when(t == last)
    def _emit_conv_state():
        cbs_ref[0] = ext_scr[2 * n_tiles - (CONV_W - 1):2 * n_tiles, SUBLANES - 1, :]

    carry_scr[...] = ext_scr[n_tiles:2 * n_tiles, SUBLANES - 1, :]

    tiles = CONV_TILES
    lane_chunk = LANES

    def _conv_rows(i, carry):
        p0 = i * tiles
        for lc in range(D // lane_chunk):
            sl = slice(lc * lane_chunk, (lc + 1) * lane_chunk)
            acc = jnp.broadcast_to(cb_ref[:, sl], (tiles * SUBLANES, lane_chunk))
            for j in range(CONV_W):
                back = CONV_W - 1 - j
                w = jnp.concatenate([cw_ref[j, :, sl]] * tiles, axis=0)
                win = ext_scr[pl.ds(n_tiles + p0 - back, tiles), :, sl]
                acc = acc + win.reshape(tiles * SUBLANES, lane_chunk) * w
            c_scr[pl.ds(p0, tiles), :, sl] = acc.reshape(tiles, SUBLANES, lane_chunk)
        return carry

    lax.fori_loop(0, n_tiles // tiles, _conv_rows, 0)

    y = _layernorm(c_scr[...].reshape(tm, D), lng_ref[...], lnb_ref[...])
    c_scr[...] = _dot((y * _sigmoid(y)).astype(BF16), w_pb_ref[...]).reshape(n_tiles, SUBLANES, D)
    branch_b = jnp.concatenate(
        [c_scr[(n * SUBLANES) % n_tiles:(n * SUBLANES) % n_tiles + SUBLANES, (n * SUBLANES) // n_tiles, :]
         for n in range(n_tiles)], axis=0)

    merged = []
    for c0 in range(0, D, COL_PIECE):
        ga = _dot(xn, w_in_ref[:, GATE_OFF + c0:GATE_OFF + c0 + COL_PIECE])
        gb = _dot(xn, w_in_ref[:, GATE_OFF + D + c0:GATE_OFF + D + c0 + COL_PIECE])
        merged.append((_sigmoid(ga) * branch_a[:, c0:c0 + COL_PIECE]
                       + _sigmoid(gb) * branch_b[:, c0:c0 + COL_PIECE]).astype(BF16))
    h_ref[0] = x + _dot(jnp.concatenate(merged, axis=1), w_out_ref[...])


def _prompt_stage_a(x, g1, w_in, sinks, w_o, cw, cb, lng, lnb, w_pb, w_out, *, tm):
    b, s, _ = x.shape
    grid = (b, s // tm)
    row_block = pl.BlockSpec((1, tm, D), lambda i, t: (i, t, 0))
    per_seq = lambda r, c: pl.BlockSpec((1, r, c), lambda i, t: (i, 0, 0))
    return pl.pallas_call(
        functools.partial(_prompt_a_body, tm=tm),
        grid=grid,
        in_specs=[row_block, _resident((1, D)), _resident((D, IN_DIM)),
                  pl.BlockSpec(memory_space=pltpu.SMEM),
                  _resident((D, D)), _resident((CONV_W, SUBLANES, D)), _resident((1, D)), _resident((1, D)),
                  _resident((1, D)), _resident((D, D)), _resident((D, D))],
        out_specs=[row_block, per_seq(WINDOW, KV_DIM), per_seq(WINDOW, KV_DIM), per_seq(CONV_W - 1, D)],
        out_shape=[jax.ShapeDtypeStruct((b, s, D), F32),
                   jax.ShapeDtypeStruct((b, WINDOW, KV_DIM), F32),
                   jax.ShapeDtypeStruct((b, WINDOW, KV_DIM), F32),
                   jax.ShapeDtypeStruct((b, CONV_W - 1, D), F32)],
        scratch_shapes=[pltpu.VMEM((N_KV, 2, WINDOW + tm, LANES), BF16),
                        pltpu.VMEM((N_KV, 2, WINDOW + tm, LANES), BF16),
                        pltpu.VMEM((tm, D), BF16),
                        pltpu.VMEM((tm, D), BF16),
                        pltpu.VMEM((SUBLANES, tm // SUBLANES, D), F32),
                        pltpu.VMEM((2 * tm // SUBLANES, SUBLANES, D), F32),
                        pltpu.VMEM((tm // SUBLANES, D), F32),
                        pltpu.VMEM((tm // SUBLANES, SUBLANES, D), F32)],
        compiler_params=pltpu.CompilerParams(dimension_semantics=("arbitrary", "arbitrary"),
                                             vmem_limit_bytes=VMEM_LIMIT),
        name="prompt_stage_a",
    )(x, g1, w_in, sinks, w_o, cw, cb, lng, lnb, w_pb, w_out)


def _ffn_chunks(h, hn, prev_rows, w_up_ref, fcw_ref, fcb_ref, wd_ref):
    def up_chunk(c):
        halves = []
        for part in range(2):
            cols = slice(part * D_FF + c * FFN_CH, part * D_FF + (c + 1) * FFN_CH)
            up = _dot(hn, w_up_ref[:, cols])
            prev1, prev2 = prev_rows(cols, up)
            w = fcw_ref[:, cols]
            halves.append(w[0:1, :] * prev2 + w[1:2, :] * prev1 + w[2:3, :] * up + fcb_ref[:, cols])
        return halves

    acts = []
    for c in range(FFN_NC):
        gate, val = up_chunk(c)
        acts.append((gate * _sigmoid(gate) * val).astype(BF16))
    return h + _dot(jnp.concatenate(acts, axis=1), wd_ref[...])


def _prompt_ffn_body(h_ref, g2_ref, w_up_ref, fcw_ref, fcb_ref, wd_ref, gf_ref,
                     y_ref, state_ref, carry_scr, *, tm):
    t = pl.program_id(1)

    @pl.when(t == 0)
    def _zero_history():
        carry_scr[...] = jnp.zeros(carry_scr.shape, F32)

    h = h_ref[0]
    hn = _rmsnorm(h, g2_ref[...]).astype(BF16)

    def prev_rows(cols, up):
        n_t = tm // SUBLANES
        ext = jnp.concatenate([carry_scr[:, cols], up], axis=0).reshape(n_t + 1, SUBLANES, FFN_CH)
        carry_scr[:, cols] = up[tm - SUBLANES:, :]
        state_ref[0, :, cols] = up[tm - (FFN_CONV_W - 1):, :]
        sub = lax.broadcasted_iota(jnp.int32, (n_t, SUBLANES, FFN_CH), 1)
        shifted = []
        for k in (1, 2):
            r = pltpu.roll(ext, k, axis=1)
            shifted.append(jnp.where(sub < k, r[0:n_t], r[1:n_t + 1]).reshape(tm, FFN_CH))
        return shifted[0], shifted[1]

    y = _ffn_chunks(h, hn, prev_rows, w_up_ref, fcw_ref, fcb_ref, wd_ref)
    y_ref[0] = _rmsnorm(y, gf_ref[...])


def _ffn_weight_specs():
    return [_resident((1, D)), _resident((D, 2 * D_FF)), _resident((FFN_CONV_W, 2 * D_FF)),
            _resident((1, 2 * D_FF)), _resident((D_FF, D)), _resident((1, D))]


def _prompt_ffn(h, ffn_w, *, tm):
    b, s, _ = h.shape
    row_block = pl.BlockSpec((1, tm, D), lambda i, t: (i, t, 0))
    state_shape = (b, FFN_CONV_W - 1, 2 * D_FF)
    return pl.pallas_call(
        functools.partial(_prompt_ffn_body, tm=tm),
        grid=(b, s // tm),
        in_specs=[row_block] + _ffn_weight_specs(),
        out_specs=[row_block, pl.BlockSpec((1,) + state_shape[1:], lambda i, t: (i, 0, 0))],
        out_shape=[jax.ShapeDtypeStruct((b, s, D), F32), jax.ShapeDtypeStruct(state_shape, F32)],
        scratch_shapes=[pltpu.VMEM((SUBLANES, 2 * D_FF), F32)],
        compiler_params=pltpu.CompilerParams(dimension_semantics=("arbitrary", "arbitrary"),
                                             vmem_limit_bytes=VMEM_LIMIT),
        name="prompt_ffn",
    )(h, *ffn_w)


def _sample_proj_body(x_ref, g1_ref, w_in_ref, q_ref, kvt_ref, u_ref, gates_ref):
    xn = _rmsnorm(x_ref[...], g1_ref[...]).astype(BF16)
    q_ref[...] = _dot(xn, w_in_ref[:, 0:D]) * SCALE
    kvt_ref[...] = lax.dot_general(w_in_ref[:, D:QKV_W], xn, (((0,), (1,)), ((), ())),
                                   preferred_element_type=F32)
    glu = _dot(xn, w_in_ref[:, GLU_OFF:GATE_OFF])
    u_ref[...] = glu[:, 0:D] * _sigmoid(glu[:, D:2 * D])
    gates_ref[...] = _sigmoid(_dot(xn, w_in_ref[:, GATE_OFF:IN_DIM]))


def _sample_proj(x2d, g1, w_in):
    m = x2d.shape[0]
    rows = m // SAMPLE_ROW_STEPS
    blk = lambda c: pl.BlockSpec((rows, c), lambda i: (i, 0))
    return pl.pallas_call(
        _sample_proj_body,
        grid=(SAMPLE_ROW_STEPS,),
        in_specs=[blk(D), _resident((1, D)), _resident((D, IN_DIM))],
        out_specs=[blk(D), pl.BlockSpec((2 * KV_DIM, rows), lambda i: (0, i)), blk(D), blk(2 * D)],
        out_shape=[jax.ShapeDtypeStruct((m, D), F32), jax.ShapeDtypeStruct((2 * KV_DIM, m), F32),
                   jax.ShapeDtypeStruct((m, D), F32), jax.ShapeDtypeStruct((m, 2 * D), F32)],
        compiler_params=pltpu.CompilerParams(dimension_semantics=("arbitrary",),
                                             vmem_limit_bytes=VMEM_LIMIT),
        name="sample_proj",
    )(x2d, g1, w_in)


def _sample_mix_body(q_ref, kvt_ref, u_ref, kt_ref, vt_ref, cstate_ref, sink_ref,
                     cw_ref, cb_ref, lng_ref, lnb_ref,
                     attn_ref, cs_ref, kt_out_ref, vt_out_ref, cstate_out_ref,
                     qg_scr, *, gs, n_tok):
    rows_per_group = GROUP * n_tok
    keep = WINDOW - n_tok

    lane = lax.broadcasted_iota(jnp.int32, (N_KV, HEAD_DIM, WINDOW), 2)
    for n in range(gs):
        shift = (keep - n_tok * n) % WINDOW
        new_k = pltpu.roll(kvt_ref[0:KV_DIM, :], shift, axis=1).reshape(N_KV, HEAD_DIM, WINDOW)
        new_v = pltpu.roll(kvt_ref[KV_DIM:2 * KV_DIM, :], shift, axis=1).reshape(N_KV, HEAD_DIM, WINDOW)
        kt_out_ref[n] = jnp.where(lane >= keep, new_k, pltpu.roll(kt_ref[n], keep, axis=2))
        vt_out_ref[n] = jnp.where(lane >= keep, new_v, pltpu.roll(vt_ref[n], keep, axis=2))

    tok = lax.broadcasted_iota(jnp.int32, (rows_per_group, WINDOW), 0) % n_tok
    key = lax.broadcasted_iota(jnp.int32, (rows_per_group, WINDOW), 1)
    mask_old = (key > tok)[None]
    mask_new = jnp.logical_and(key >= keep, key - keep <= tok)[None]
    for g in range(N_KV):
        for hh in range(GROUP):
            h0 = (GROUP * g + hh) * HEAD_DIM
            qg_scr[:, hh * n_tok:(hh + 1) * n_tok, :] = q_ref[:, h0:h0 + HEAD_DIM].reshape(gs, n_tok, HEAD_DIM)
        qg = qg_scr[...].astype(BF16)
        s_old = jnp.einsum("nqd,ndk->nqk", qg, kt_ref[:, g].astype(BF16), preferred_element_type=F32)
        s_new = jnp.einsum("nqd,ndk->nqk", qg, kt_out_ref[:, g].astype(BF16), preferred_element_type=F32)
        s_old = jnp.where(mask_old, s_old, NEG_INF)
        s_new = jnp.where(mask_new, s_new, NEG_INF)
        sink = sink_ref[g][None]
        m = jnp.maximum(jnp.maximum(jnp.max(s_old, axis=-1, keepdims=True),
                                    jnp.max(s_new, axis=-1, keepdims=True)), sink)
        e_old = jnp.exp(s_old - m)
        e_new = jnp.exp(s_new - m)
        den = (jnp.sum(e_old, axis=-1, keepdims=True) + jnp.sum(e_new, axis=-1, keepdims=True)
               + jnp.exp(sink - m))
        o = (jnp.einsum("nqk,ndk->nqd", (e_old / den).astype(BF16), vt_ref[:, g].astype(BF16),
                        preferred_element_type=F32)
             + jnp.einsum("nqk,ndk->nqd", (e_new / den).astype(BF16), vt_out_ref[:, g].astype(BF16),
                          preferred_element_type=F32))
        for hh in range(GROUP):
            h0 = (GROUP * g + hh) * HEAD_DIM
            attn_ref[:, h0:h0 + HEAD_DIM] = o[:, hh * n_tok:(hh + 1) * n_tok, :].reshape(gs * n_tok, HEAD_DIM)

    n_state = CONV_W - 1
    u_slabs = [u_ref[:, i, :] for i in range(n_tok)]

    def ext(r, sl):
        return cstate_ref[r, :, sl] if r < n_state else u_slabs[r - n_state][:, sl]

    for r in range(n_state):
        cstate_out_ref[r] = ext(r + n_tok, slice(None))

    lane_chunk = 2 * LANES
    for i in range(n_tok):
        pieces = []
        for lc in range(D // lane_chunk):
            sl = slice(lc * lane_chunk, (lc + 1) * lane_chunk)
            acc = jnp.broadcast_to(cb_ref[:, sl], (gs, lane_chunk))
            for j in range(CONV_W):
                w = jnp.concatenate([cw_ref[j, :, sl]] * (gs // SUBLANES), axis=0)
                acc = acc + ext(i + j, sl) * w
            pieces.append(acc)
        y = _layernorm(jnp.concatenate(pieces, axis=1), lng_ref[...], lnb_ref[...])
        cs_ref[:, i, :] = y * _sigmoid(y)


def _sample_mix(q, kvt, u, kt, vt, cstate, sink_rows, cw, cb, lng, lnb, *, gs):
    n_seq, n_tok, _ = u.shape
    rows = gs * n_tok
    assert rows % LANES == 0, "a group's token columns must fill whole lane tiles of the transposed k/v"
    row_block = pl.BlockSpec((rows, D), lambda s: (s, 0))
    seq_block = pl.BlockSpec((gs, n_tok, D), lambda s: (s, 0, 0))
    cache_block = pl.BlockSpec((gs, N_KV, HEAD_DIM, WINDOW), lambda s: (s, 0, 0, 0))
    state_block = pl.BlockSpec((CONV_W - 1, gs, D), lambda s: (0, s, 0))
    rows_per_group = GROUP * n_tok
    return pl.pallas_call(
        functools.partial(_sample_mix_body, gs=gs, n_tok=n_tok),
        grid=(n_seq // gs,),
        in_specs=[row_block, pl.BlockSpec((2 * KV_DIM, rows), lambda s: (0, s)), seq_block,
                  cache_block, cache_block, state_block,
                  _resident((N_KV, rows_per_group, 1)),
                  _resident((CONV_W, SUBLANES, D)), _resident((1, D)), _resident((1, D)), _resident((1, D))],
        out_specs=[row_block, seq_block, cache_block, cache_block, state_block],
        out_shape=[jax.ShapeDtypeStruct((n_seq * n_tok, D), F32), jax.ShapeDtypeStruct((n_seq, n_tok, D), F32),
                   jax.ShapeDtypeStruct((n_seq, N_KV, HEAD_DIM, WINDOW), F32),
                   jax.ShapeDtypeStruct((n_seq, N_KV, HEAD_DIM, WINDOW), F32),
                   jax.ShapeDtypeStruct((CONV_W - 1, n_seq, D), F32)],
        scratch_shapes=[pltpu.VMEM((gs, rows_per_group, HEAD_DIM), F32)],
        compiler_params=pltpu.CompilerParams(dimension_semantics=("arbitrary",),
                                             vmem_limit_bytes=VMEM_LIMIT),
        name="sample_mix",
    )(q, kvt, u, kt, vt, cstate, sink_rows, cw, cb, lng, lnb)


def _sample_merge_body(x_ref, attn_ref, cs_ref, gates_ref, w_o_ref, w_pb_ref, w_out_ref, h_ref):
    x = x_ref[...]
    branch_a = _dot(attn_ref[...].astype(BF16), w_o_ref[...])
    branch_b = _dot(cs_ref[...].astype(BF16), w_pb_ref[...])
    merged = gates_ref[:, 0:D] * branch_a + gates_ref[:, D:2 * D] * branch_b
    h_ref[...] = x + _dot(merged.astype(BF16), w_out_ref[...])


def _sample_merge(x, attn, cs, gates, w_o, w_pb, w_out):
    m = x.shape[0]
    blk = lambda c: pl.BlockSpec((m // SAMPLE_ROW_STEPS, c), lambda i: (i, 0))
    return pl.pallas_call(
        _sample_merge_body,
        grid=(SAMPLE_ROW_STEPS,),
        in_specs=[blk(D), blk(D), blk(D), blk(2 * D),
                  _resident((D, D)), _resident((D, D)), _resident((D, D))],
        out_specs=blk(D),
        out_shape=jax.ShapeDtypeStruct((m, D), F32),
        compiler_params=pltpu.CompilerParams(dimension_semantics=("arbitrary",),
                                             vmem_limit_bytes=VMEM_LIMIT),
        name="sample_merge",
    )(x, attn, cs, gates, w_o, w_pb, w_out)


def _sample_ffn_body(h_ref, state_ref, g2_ref, w_up_ref, fcw_ref, fcb_ref, wd_ref, gf_ref,
                     y_ref, state_out_ref, *, n_seq, n_tok):
    m = n_seq * n_tok
    h = jnp.concatenate([h_ref[:, i, :] for i in range(n_tok)], axis=0)
    hn = _rmsnorm(h, g2_ref[...]).astype(BF16)

    def prev_rows(cols, up):
        s0 = state_ref[0, :, cols]
        s1 = state_ref[1, :, cols]
        state_out_ref[0, :, cols] = up[m - 2 * n_seq:m - n_seq, :]
        state_out_ref[1, :, cols] = up[m - n_seq:, :]
        prev1 = jnp.concatenate([s1, up[:m - n_seq, :]], axis=0)
        prev2 = jnp.concatenate([s0, s1, up[:m - 2 * n_seq, :]], axis=0)
        return prev1, prev2

    y = _ffn_chunks(h, hn, prev_rows, w_up_ref, fcw_ref, fcb_ref, wd_ref)
    y = _rmsnorm(y, gf_ref[...])
    for i in range(n_tok):
        y_ref[:, i, :] = y[i * n_seq:(i + 1) * n_seq, :]


def _sample_ffn(h, state, ffn_w, *, n_seq, n_tok):
    m = n_seq * n_tok
    state_shape = (FFN_CONV_W - 1, n_seq, 2 * D_FF)
    state_spec = pl.BlockSpec(state_shape, lambda i: (0, 0, 0))
    seq_major = pl.BlockSpec((n_seq, n_tok, D), lambda i: (0, 0, 0))
    return pl.pallas_call(
        functools.partial(_sample_ffn_body, n_seq=n_seq, n_tok=n_tok),
        grid=(1,),
        in_specs=[seq_major, state_spec] + _ffn_weight_specs(),
        out_specs=[seq_major, state_spec],
        out_shape=[jax.ShapeDtypeStruct((n_seq, n_tok, D), F32), jax.ShapeDtypeStruct(state_shape, F32)],
        compiler_params=pltpu.CompilerParams(dimension_semantics=("arbitrary",),
                                             vmem_limit_bytes=VMEM_LIMIT),
        name="sample_ffn",
    )(h, state, *ffn_w)


def _layer(x_prompt, x_sample, k_past, v_past, conv_b_past, conv_f_past,
           norm1_g, w_in, sinks, w_o_attn, conv_b_w, conv_b_bias, ln_b_g, ln_b_b, w_pb, w_out,
           norm2_g, w_up, ffn_conv_w, ffn_conv_b, w_down, norm_f_g, *, tm_a, tm_ffn, gs):
    n_seq, n_tok, _ = x_sample.shape
    row = lambda v: v.reshape(1, -1)
    g1, g2, gf = row(norm1_g), row(norm2_g), row(norm_f_g)
    cb, lng, lnb = row(conv_b_bias), row(ln_b_g), row(ln_b_b)
    w_in_b, w_o_b, w_pb_b, w_out_b = (w.astype(BF16) for w in (w_in, w_o_attn, w_pb, w_out))
    ffn_w = (g2, w_up.astype(BF16), ffn_conv_w, row(ffn_conv_b), w_down.astype(BF16), gf)

    cw8 = jnp.broadcast_to(conv_b_w[:, None, :], (CONV_W, SUBLANES, D))
    h_p, k_win_p, v_win_p, conv_b_p = _prompt_stage_a(
        x_prompt, g1, w_in_b, sinks, w_o_b, cw8, cb, lng, lnb, w_pb_b, w_out_b, tm=tm_a)
    y_p, conv_f_p = _prompt_ffn(h_p, ffn_w, tm=tm_ffn)

    m = n_seq * n_tok
    q, kvt, u, gates = _sample_proj(x_sample.reshape(m, D), g1, w_in_b)
    sink_rows = jnp.repeat(sinks.reshape(N_KV, GROUP), n_tok, axis=1)[:, :, None]
    attn, cs, kt_s, vt_s, cstate_s = _sample_mix(
        q, kvt, u.reshape(n_seq, n_tok, D),
        k_past.transpose(0, 2, 3, 1), v_past.transpose(0, 2, 3, 1), conv_b_past.transpose(1, 0, 2),
        sink_rows, cw8, cb, lng, lnb, gs=gs)
    conv_b_s = cstate_s.transpose(1, 0, 2)
    h_s = _sample_merge(x_sample.reshape(m, D), attn, cs.reshape(m, D), gates, w_o_b, w_pb_b, w_out_b)
    y_s, ffn_state_s = _sample_ffn(h_s.reshape(n_seq, n_tok, D), conv_f_past.transpose(1, 0, 2), ffn_w,
                                   n_seq=n_seq, n_tok=n_tok)
    conv_f_s = ffn_state_s.transpose(1, 0, 2)

    kv_shape = lambda a: a.reshape(a.shape[0], WINDOW, N_KV, HEAD_DIM)
    return (y_p, y_s,
            kv_shape(k_win_p), kv_shape(v_win_p), conv_b_p, conv_f_p,
            kt_s.transpose(0, 3, 1, 2), vt_s.transpose(0, 3, 1, 2), conv_b_s, conv_f_s)


def kernel(x_prompt, x_sample, cache_k_win, cache_v_win, state_conv_b, state_conv_ffn, norm1_g, w_in, sinks,
           w_o_attn, conv_b_w, conv_b_bias, ln_b_g, ln_b_b, w_pb, w_out, norm2_g, w_up, ffn_conv_w,
           ffn_conv_b, w_down, norm_f_g):
    assert w_in.shape[0] == 1, "single-layer stack only"
    res = _layer(x_prompt, x_sample, cache_k_win[0], cache_v_win[0], state_conv_b[0], state_conv_ffn[0],
                 norm1_g[0], w_in[0], sinks[0], w_o_attn[0], conv_b_w[0], conv_b_bias[0], ln_b_g[0],
                 ln_b_b[0], w_pb[0], w_out[0], norm2_g[0], w_up[0], ffn_conv_w[0], ffn_conv_b[0],
                 w_down[0], norm_f_g, tm_a=PROMPT_TILE, tm_ffn=FFN_TILE, gs=SAMPLE_GROUP)
    return res[:2] + tuple(r[None] for r in res[2:])
```
